```python
import jax, jax.numpy as jnp
from jax import lax
import numpy as np

D_MODEL = 2048
BATCH = 1
SEQ = 16384
DEPTH = 4
DEC_BATCH = 2
DEC_SEQ = 4096
PAST_LEN = 128

N_MIXERS = 2
N_A_LAYERS = (DEPTH + 1) // 2
N_B_LAYERS = DEPTH // 2
MLA_HEADS = 16
Q_LORA = 512
KV_LORA = 512
NOPE_DIM = 128
ROPE_DIM = 64
V_DIM = 128
ROPE_THETA = 10000.0
Q_BLOCK = 128
SWA_Q_HEADS = 16
SWA_KV_HEADS = 4
SWA_GROUP = SWA_Q_HEADS // SWA_KV_HEADS
SWA_HEAD_DIM = 128
WINDOW = 128
BLOCK = 128
N_BUCKETS = 32
MAX_DISTANCE = 128
D_FF = 4 * D_MODEL
PLE_DIM = 256
EPS = 1e-6
NEG_INF = -1e30

kernel_name = 'hybrid_mla_swa_sink_encoder'


def rmsnorm(x, g):
    xf = x.astype(jnp.float32)
    y = xf * lax.rsqrt(jnp.mean(xf * xf, axis=-1, keepdims=True) + EPS)
    return (y * g.astype(jnp.float32)).astype(x.dtype)


def rotary(x, pos):
    half = ROPE_DIM // 2
    inv = 1.0 / (ROPE_THETA ** (jnp.arange(half, dtype=jnp.float32) / half))
    ang = pos.astype(jnp.float32)[:, None] * inv[None, :]
    cos = jnp.cos(ang)[None, :, None, :]
    sin = jnp.sin(ang)[None, :, None, :]
    xf = x.astype(jnp.float32)
    x1, x2 = xf[..., :half], xf[..., half:]
    return jnp.concatenate([x1 * cos - x2 * sin, x2 * cos + x1 * sin], axis=-1).astype(x.dtype)


def t5_bucket(rel):
    nb = N_BUCKETS // 2
    max_exact = nb // 2
    ret = (rel > 0).astype(np.int32) * nb
    n = np.abs(rel)
    large = max_exact + (np.log(np.maximum(n, 1).astype(np.float32) / max_exact)
                         / np.log(MAX_DISTANCE / max_exact) * (nb - max_exact)).astype(np.int32)
    large = np.minimum(large, nb - 1)
    return (ret + np.where(n < max_exact, n, large)).astype(np.int32)


def mla(x, w_down, q_norm, kv_norm, w_uq, w_ukv, w_o):
    B, S, _ = x.shape
    H = MLA_HEADS
    lat = x @ w_down
    c_q = rmsnorm(lat[..., :Q_LORA], q_norm)
    c_kv = rmsnorm(lat[..., Q_LORA:Q_LORA + KV_LORA], kv_norm)
    k_rope = lat[..., Q_LORA + KV_LORA:]
    q = (c_q @ w_uq).reshape(B, S, H, NOPE_DIM + ROPE_DIM)
    kv = (c_kv @ w_ukv).reshape(B, S, H, NOPE_DIM + V_DIM)
    pos = jnp.arange(S)
    q_nope = q[..., :NOPE_DIM]
    q_rope = rotary(q[..., NOPE_DIM:], pos)
    k_nope = kv[..., :NOPE_DIM]
    v = kv[..., NOPE_DIM:]
    k_rope = rotary(k_rope[:, :, None, :], pos)[:, :, 0, :]
    scale = (NOPE_DIM + ROPE_DIM) ** -0.5
    nq = S // Q_BLOCK

    def to_blocks(t):
        return t.reshape(B, nq, Q_BLOCK, *t.shape[2:]).swapaxes(0, 1)

    def attend(blk):
        qn, qr = blk
        s = (jnp.einsum('bqhd,bkhd->bhqk', qn, k_nope).astype(jnp.float32)
             + jnp.einsum('bqhr,bkr->bhqk', qr, k_rope).astype(jnp.float32)) * scale
        p = jax.nn.softmax(s, axis=-1).astype(v.dtype)
        return jnp.einsum('bhqk,bkhd->bqhd', p, v)

    o = lax.map(attend, (to_blocks(q_nope), to_blocks(q_rope)))
    o = o.swapaxes(0, 1).reshape(B, S, H * V_DIM)
    return o @ w_o


def swa(x, w_qkv, sink, rel_bias, w_o):
    B, S, _ = x.shape
    nb = S // BLOCK
    HQ, HKV, G, DH = SWA_Q_HEADS, SWA_KV_HEADS, SWA_GROUP, SWA_HEAD_DIM
    qkv = x @ w_qkv
    q = qkv[..., :HQ * DH].reshape(B, nb, BLOCK, HKV, G, DH)
    k = qkv[..., HQ * DH:(HQ + HKV) * DH].reshape(B, S, HKV, DH)
    v = qkv[..., (HQ + HKV) * DH:].reshape(B, S, HKV, DH)

    def band(t):
        tp = jnp.pad(t, ((0, 0), (BLOCK, BLOCK), (0, 0), (0, 0))).reshape(B, nb + 2, BLOCK, HKV, DH)
        return jnp.concatenate([tp[:, :-2], tp[:, 1:-1], tp[:, 2:]], axis=2)

    kb, vb = band(k), band(v)
    s = jnp.einsum('bnqkgd,bnskd->bnkgqs', q, kb).astype(jnp.float32) * (DH ** -0.5)
    qi = np.arange(BLOCK)[:, None]
    si = np.arange(3 * BLOCK)[None, :]
    rel = si - BLOCK - qi
    bias = rel_bias[t5_bucket(rel)].astype(jnp.float32)
    bias = bias.transpose(2, 0, 1).reshape(HKV, G, BLOCK, 3 * BLOCK)
    kpos = np.arange(nb)[:, None] * BLOCK + np.arange(3 * BLOCK)[None, :] - BLOCK
    mask = (np.abs(rel) <= WINDOW)[None] & ((kpos >= 0) & (kpos < S))[:, None, :]
    s = jnp.where(mask[None, :, None, None], s + bias, NEG_INF)
    sk = sink.astype(jnp.float32).reshape(HKV, G, 1, 1)
    m = jnp.maximum(jnp.max(s, axis=-1, keepdims=True), sk)
    e = jnp.exp(s - m)
    p = e / (jnp.sum(e, axis=-1, keepdims=True) + jnp.exp(sk - m))
    o = jnp.einsum('bnkgqs,bnskd->bnqkgd', p.astype(v.dtype), vb).reshape(B, S, HQ * DH)
    return o @ w_o


def sq_relu_mlp(x, w_up, w_down):
    h = jnp.maximum(x @ w_up, 0)
    return (h * h) @ w_down


def trunk(x, p, norm_gains, mla_w_down, mla_q_norm, mla_kv_norm, mla_w_uq, mla_w_ukv, mla_w_o,
          swa_w_qkv, swa_sink, swa_w_o, rel_bias, mlp_w_up, mlp_w_down, ple_w_up, ple_w_gate, ple_norm):
    for i in range(DEPTH):
        g = norm_gains[i]
        j = i // N_MIXERS
        h = rmsnorm(x, g[0])
        if i % N_MIXERS == 0:
            h = mla(h, mla_w_down[j], mla_q_norm[j], mla_kv_norm[j], mla_w_uq[j], mla_w_ukv[j], mla_w_o[j])
        else:
            h = swa(h, swa_w_qkv[j], swa_sink[j], rel_bias, swa_w_o[j])
        x = x + rmsnorm(h, g[1])
        h = sq_relu_mlp(rmsnorm(x, g[2]), mlp_w_up[i], mlp_w_down[i])
        x = x + rmsnorm(h, g[3])
        e = rmsnorm(p[i] @ ple_w_up[i], ple_norm[i])
        x = x + e * jax.nn.sigmoid(x @ ple_w_gate[i])
    return x


def setup_inputs(seed: int = 0) -> dict:
    key = jax.random.key(seed)
    ks = jax.random.split(key, 24)
    f32 = jnp.float32

    def nrm(k, shape, scale):
        return jax.random.normal(k, shape, f32) * scale

    def gain(k, shape):
        return 1.0 + 0.02 * jax.random.normal(k, shape, f32)

    H = MLA_HEADS
    return {
        'x_prompt': nrm(ks[0], (BATCH, SEQ, D_MODEL), 1.0),
        'x_sample': nrm(ks[1], (DEC_BATCH, DEC_SEQ, D_MODEL), 1.0),
        'p_prompt': nrm(ks[2], (DEPTH, BATCH, SEQ, PLE_DIM), 1.0),
        'p_sample': nrm(ks[3], (DEPTH, DEC_BATCH, DEC_SEQ, PLE_DIM), 1.0),
        'norm_gains': gain(ks[4], (DEPTH, 4, D_MODEL)),
        'mla_w_down': nrm(ks[5], (N_A_LAYERS, D_MODEL, Q_LORA + KV_LORA + ROPE_DIM), D_MODEL ** -0.5),
        'mla_q_norm': gain(ks[6], (N_A_LAYERS, Q_LORA)),
        'mla_kv_norm': gain(ks[7], (N_A_LAYERS, KV_LORA)),
        'mla_w_uq': nrm(ks[8], (N_A_LAYERS, Q_LORA, H * (NOPE_DIM + ROPE_DIM)), Q_LORA ** -0.5),
        'mla_w_ukv': nrm(ks[9], (N_A_LAYERS, KV_LORA, H * (NOPE_DIM + V_DIM)), KV_LORA ** -0.5),
        'mla_w_o': nrm(ks[10], (N_A_LAYERS, H * V_DIM, D_MODEL), (H * V_DIM) ** -0.5),
        'swa_w_qkv': nrm(ks[11], (N_B_LAYERS, D_MODEL, (SWA_Q_HEADS + 2 * SWA_KV_HEADS) * SWA_HEAD_DIM), D_MODEL ** -0.5),
        'swa_sink': nrm(ks[12], (N_B_LAYERS, SWA_Q_HEADS), 0.5),
        'swa_w_o': nrm(ks[13], (N_B_LAYERS, SWA_Q_HEADS * SWA_HEAD_DIM, D_MODEL), (SWA_Q_HEADS * SWA_HEAD_DIM) ** -0.5),
        'rel_bias': nrm(ks[14], (N_BUCKETS, SWA_Q_HEADS), 0.5),
        'mlp_w_up': nrm(ks[15], (DEPTH, D_MODEL, D_FF), D_MODEL ** -0.5),
        'mlp_w_down': nrm(ks[16], (DEPTH, D_FF, D_MODEL), D_FF ** -0.5),
        'ple_w_up': nrm(ks[17], (DEPTH, PLE_DIM, D_MODEL), PLE_DIM ** -0.5),
        'ple_w_gate': nrm(ks[18], (DEPTH, D_MODEL, D_MODEL), D_MODEL ** -0.5),
        'ple_norm': gain(ks[19], (DEPTH, D_MODEL)),
    }


def reference(x_prompt, x_sample, p_prompt, p_sample, norm_gains, mla_w_down, mla_q_norm, mla_kv_norm,
              mla_w_uq, mla_w_ukv, mla_w_o, swa_w_qkv, swa_sink, swa_w_o, rel_bias, mlp_w_up, mlp_w_down,
              ple_w_up, ple_w_gate, ple_norm):
    y_prompt = trunk(x_prompt, p_prompt, norm_gains, mla_w_down, mla_q_norm, mla_kv_norm, mla_w_uq, mla_w_ukv,
                     mla_w_o, swa_w_qkv, swa_sink, swa_w_o, rel_bias, mlp_w_up, mlp_w_down, ple_w_up,
                     ple_w_gate, ple_norm)
    y_sample = trunk(x_sample, p_sample, norm_gains, mla_w_down, mla_q_norm, mla_kv_norm, mla_w_uq, mla_w_ukv,
                     mla_w_o, swa_w_qkv, swa_sink, swa_w_o, rel_bias, mlp_w_up, mlp_w_down, ple_w_up,
                     ple_w_gate, ple_norm)
    return (y_prompt, y_sample)
```

```python
import functools

import numpy as np
import jax
import jax.numpy as jnp
from jax import lax
from jax.experimental import pallas as pl
from jax.experimental.pallas import tpu as pltpu

D_MODEL = 2048
DEPTH = 4
N_MIXERS = 2
MLA_HEADS = 16
Q_LORA = 512
KV_LORA = 512
NOPE_DIM = 128
ROPE_DIM = 64
V_DIM = 128
ROPE_THETA = 10000.0
SWA_Q_HEADS = 16
SWA_KV_HEADS = 4
SWA_GROUP = SWA_Q_HEADS // SWA_KV_HEADS
SWA_HEAD_DIM = 128
WINDOW = 128
BLOCK = 128
N_BUCKETS = 32
MAX_DISTANCE = 128
D_FF = 4 * D_MODEL
PLE_DIM = 256
EPS = 1e-6
NEG_INF = -1e30

LANES = 128
QK_PAD = 2 * LANES
VMEM_LIMIT = 56 * 1024 * 1024

F32 = jnp.float32
BF16 = jnp.bfloat16


def _rms(x, g):
    return x * lax.rsqrt(jnp.mean(x * x, axis=-1, keepdims=True) + EPS) * g


def _dot(a, b):
    return jnp.dot(a, b, preferred_element_type=F32)


def _dot_nt(a, b):
    return lax.dot_general(a, b, (((1,), (1,)), ((), ())), preferred_element_type=F32)


def _const_spec(shape):
    nd = len(shape)
    return pl.BlockSpec(shape, lambda *_: (0,) * nd)


def _params(sem):
    return pltpu.CompilerParams(dimension_semantics=sem, vmem_limit_bytes=VMEM_LIMIT)


def _norm_matmul_kernel(x_ref, g_ref, w_ref, o_ref, *, tn):
    h = _rms(x_ref[...], g_ref[...]).astype(BF16)
    for c in range(0, w_ref.shape[1], tn):
        o_ref[:, c:c + tn] = _dot(h, w_ref[:, c:c + tn]).astype(o_ref.dtype)


def _norm_matmul(x, g, w, tm=512, tn=1024):
    t, d = x.shape
    n = w.shape[1]
    return pl.pallas_call(
        functools.partial(_norm_matmul_kernel, tn=tn),
        grid=(t // tm,),
        in_specs=[pl.BlockSpec((tm, d), lambda i: (i, 0)), _const_spec((1, d)), _const_spec((d, n))],
        out_specs=pl.BlockSpec((tm, n), lambda i: (i, 0)),
        out_shape=jax.ShapeDtypeStruct((t, n), BF16),
        compiler_params=_params(("parallel",)),
        name="norm_matmul",
    )(x, g, w)


def _mla_proj_kernel(x_ref, g_ref, wd_ref, qn_ref, kvn_ref, wuq_ref, wukv_ref, cos_ref, sin_ref,
                     q_ref, k_ref, v_ref, *, scale):
    h = _rms(x_ref[...], g_ref[...]).astype(BF16)
    lat = _dot(h, wd_ref[...])
    cq = _rms(lat[:, :Q_LORA], qn_ref[...]).astype(BF16)
    ckv = _rms(lat[:, Q_LORA:Q_LORA + KV_LORA], kvn_ref[...]).astype(BF16)
    cos = cos_ref[...]
    sin = sin_ref[...]
    base = Q_LORA + KV_LORA
    kr = (lat[:, base:base + LANES] * cos + lat[:, base + LANES:base + 2 * LANES] * sin).astype(BF16)
    hn = MLA_HEADS * NOPE_DIM
    for hd in range(MLA_HEADS):
        lo = hd * LANES
        qn = _dot(cq, wuq_ref[:, lo:lo + LANES])
        qr = _dot(cq, wuq_ref[:, hn + lo:hn + lo + LANES])
        qs = _dot(cq, wuq_ref[:, 2 * hn + lo:2 * hn + lo + LANES])
        q_ref[hd, :, :LANES] = (qn * scale).astype(BF16)
        q_ref[hd, :, LANES:] = ((qr * cos + qs * sin) * scale).astype(BF16)
        kv = _dot(ckv, wukv_ref[:, 2 * lo:2 * lo + 2 * LANES])
        k_ref[hd, :, :LANES] = kv[:, :LANES].astype(BF16)
        k_ref[hd, :, LANES:] = kr
        v_ref[hd] = kv[:, LANES:].astype(BF16)


def _mla_proj(x, g, wd, qn, kvn, wuq, wukv, cos, sin, seq, tm=256):
    t, d = x.shape
    nblk = seq // tm
    scale = float((NOPE_DIM + ROPE_DIM) ** -0.5)
    heads = MLA_HEADS
    return pl.pallas_call(
        functools.partial(_mla_proj_kernel, scale=scale),
        grid=(t // tm,),
        in_specs=[
            pl.BlockSpec((tm, d), lambda i: (i, 0)),
            _const_spec((1, d)),
            _const_spec(wd.shape),
            _const_spec((1, Q_LORA)),
            _const_spec((1, KV_LORA)),
            _const_spec(wuq.shape),
            _const_spec(wukv.shape),
            pl.BlockSpec((tm, LANES), lambda i: (i % nblk, 0)),
            pl.BlockSpec((tm, LANES), lambda i: (i % nblk, 0)),
        ],
        out_specs=[
            pl.BlockSpec((heads, tm, QK_PAD), lambda i: (0, i, 0)),
            pl.BlockSpec((heads, tm, QK_PAD), lambda i: (0, i, 0)),
            pl.BlockSpec((heads, tm, V_DIM), lambda i: (0, i, 0)),
        ],
        out_shape=[
            jax.ShapeDtypeStruct((heads, t, QK_PAD), BF16),
            jax.ShapeDtypeStruct((heads, t, QK_PAD), BF16),
            jax.ShapeDtypeStruct((heads, t, V_DIM), BF16),
        ],
        compiler_params=_params(("parallel",)),
        name="mla_proj",
    )(x, g, wd, qn, kvn, wuq, wukv, cos, sin)


def _mla_attn_kernel(q_ref, k_ref, v_ref, o_ref, m_sc, l_sc, acc_sc, *, tkc):
    seq = k_ref.shape[0]
    q = q_ref[...]
    m_sc[...] = jnp.full(m_sc.shape, -jnp.inf, F32)
    l_sc[...] = jnp.zeros(l_sc.shape, F32)
    acc_sc[...] = jnp.zeros(acc_sc.shape, F32)

    def body(c, carry):
        off = pl.multiple_of(c * tkc, tkc)
        s = _dot_nt(q, k_ref[pl.ds(off, tkc), :])
        m_prev = m_sc[...]
        m_new = jnp.maximum(m_prev, jnp.max(s, axis=-1, keepdims=True))
        alpha = jnp.exp(m_prev - m_new)
        p = jnp.exp(s - m_new)
        l_sc[...] = alpha * l_sc[...] + jnp.sum(p, axis=-1, keepdims=True)
        acc_sc[...] = alpha * acc_sc[...] + _dot(p.astype(BF16), v_ref[pl.ds(off, tkc), :])
        m_sc[...] = m_new
        return carry

    lax.fori_loop(0, seq // tkc, body, 0)
    o_ref[...] = (acc_sc[...] / l_sc[...]).astype(o_ref.dtype)


def _mla_attn(q, k, v, batch, seq, tq=512, tkc=512):
    heads, t, _ = q.shape
    nq = seq // tq
    return pl.pallas_call(
        functools.partial(_mla_attn_kernel, tkc=tkc),
        grid=(batch, heads, nq),
        in_specs=[
            pl.BlockSpec((None, tq, QK_PAD), lambda b, h, i: (h, b * nq + i, 0)),
            pl.BlockSpec((None, seq, QK_PAD), lambda b, h, i: (h, b, 0)),
            pl.BlockSpec((None, seq, V_DIM), lambda b, h, i: (h, b, 0)),
        ],
        out_specs=pl.BlockSpec((tq, V_DIM), lambda b, h, i: (b * nq + i, h)),
        out_shape=jax.ShapeDtypeStruct((t, heads * V_DIM), BF16),
        scratch_shapes=[
            pltpu.VMEM((tq, 1), F32),
            pltpu.VMEM((tq, 1), F32),
            pltpu.VMEM((tq, V_DIM), F32),
        ],
        compiler_params=_params(("parallel", "parallel", "arbitrary")),
        name="mla_attn",
    )(q, k, v)


def _t5_bucket(rel):
    nb = N_BUCKETS // 2
    max_exact = nb // 2
    ret = (rel > 0).astype(np.int32) * nb
    n = np.abs(rel)
    large = max_exact + (np.log(np.maximum(n, 1).astype(np.float32) / max_exact)
                         / np.log(MAX_DISTANCE / max_exact) * (nb - max_exact)).astype(np.int32)
    large = np.minimum(large, nb - 1)
    return (ret + np.where(n < max_exact, n, large)).astype(np.int32)


def _bias_table_kernel(rb_ref, bucket_ref, o_ref):
    hd = pl.program_id(0)
    bucket = bucket_ref[...]
    acc = jnp.zeros(bucket.shape, F32)
    for b in range(N_BUCKETS):
        acc = jnp.where(bucket == b, rb_ref[b, hd], acc)
    o_ref[...] = acc


def _bias_table(rel_bias):
    qi = np.arange(BLOCK)[:, None]
    si = np.arange(3 * BLOCK)[None, :]
    bucket = jnp.asarray(_t5_bucket(si - BLOCK - qi))
    return pl.pallas_call(
        _bias_table_kernel,
        grid=(SWA_Q_HEADS,),
        in_specs=[pl.BlockSpec(memory_space=pltpu.SMEM), _const_spec((BLOCK, 3 * BLOCK))],
        out_specs=pl.BlockSpec((None, BLOCK, 3 * BLOCK), lambda h: (h, 0, 0)),
        out_shape=jax.ShapeDtypeStruct((SWA_Q_HEADS, BLOCK, 3 * BLOCK), F32),
        name="t5_bias_table",
    )(rel_bias, bucket)


def _swa_attn_kernel(sink_ref, q_ref, kp_ref, kc_ref, kn_ref, vp_ref, vc_ref, vn_ref, bias_ref, o_ref,
                     *, nqb, nblocks):
    i = pl.program_id(1)
    dh = SWA_HEAD_DIM
    scale = float(dh ** -0.5)
    kband = jnp.concatenate([kp_ref[...], kc_ref[...], kn_ref[...]], axis=0)
    vband = jnp.concatenate([vp_ref[...], vc_ref[...], vn_ref[...]], axis=0)
    qi = lax.broadcasted_iota(jnp.int32, (BLOCK, 3 * BLOCK), 0)
    si = lax.broadcasted_iota(jnp.int32, (BLOCK, 3 * BLOCK), 1)
    rel = si - BLOCK - qi
    in_window = jnp.abs(rel) <= WINDOW
    for j in range(nqb):
        blk = i * nqb + j
        lo_ok = jnp.logical_or(si >= BLOCK, blk > 0)
        hi_ok = jnp.logical_or(si < 2 * BLOCK, blk < nblocks - 1)
        mask = in_window & lo_ok & hi_ok
        r0 = j * BLOCK
        for kh in range(SWA_KV_HEADS):
            kb = kband[r0:r0 + 3 * BLOCK, kh * dh:(kh + 1) * dh]
            vb = vband[r0:r0 + 3 * BLOCK, kh * dh:(kh + 1) * dh]
            for g in range(SWA_GROUP):
                hq = kh * SWA_GROUP + g
                qh = q_ref[r0:r0 + BLOCK, hq * dh:(hq + 1) * dh]
                s = _dot_nt(qh, kb) * scale
                s = jnp.where(mask, s + bias_ref[hq], NEG_INF)
                sk = sink_ref[hq]
                m = jnp.maximum(jnp.max(s, axis=-1, keepdims=True), sk)
                e = jnp.exp(s - m)
                p = e / (jnp.sum(e, axis=-1, keepdims=True) + jnp.exp(sk - m))
                o_ref[r0:r0 + BLOCK, hq * dh:(hq + 1) * dh] = _dot(p.astype(BF16), vb).astype(o_ref.dtype)


def _swa_attn(qkv, sink, bias, batch, seq, tb=256):
    t = qkv.shape[0]
    nqb = tb // BLOCK
    nsteps = seq // tb
    nblocks = seq // BLOCK
    hq_w = SWA_Q_HEADS * SWA_HEAD_DIM
    kv_w = SWA_KV_HEADS * SWA_HEAD_DIM
    kcol = hq_w // kv_w
    vcol = kcol + 1

    def prev_row(b, i):
        return b * nblocks + jnp.maximum(i * nqb - 1, 0)

    def next_row(b, i):
        return b * nblocks + jnp.minimum((i + 1) * nqb, nblocks - 1)

    return pl.pallas_call(
        functools.partial(_swa_attn_kernel, nqb=nqb, nblocks=nblocks),
        grid=(batch, nsteps),
        in_specs=[
            pl.BlockSpec(memory_space=pltpu.SMEM),
            pl.BlockSpec((tb, hq_w), lambda b, i: (b * nsteps + i, 0)),
            pl.BlockSpec((BLOCK, kv_w), lambda b, i: (prev_row(b, i), kcol)),
            pl.BlockSpec((tb, kv_w), lambda b, i: (b * nsteps + i, kcol)),
            pl.BlockSpec((BLOCK, kv_w), lambda b, i: (next_row(b, i), kcol)),
            pl.BlockSpec((BLOCK, kv_w), lambda b, i: (prev_row(b, i), vcol)),
            pl.BlockSpec((tb, kv_w), lambda b, i: (b * nsteps + i, vcol)),
            pl.BlockSpec((BLOCK, kv_w), lambda b, i: (next_row(b, i), vcol)),
            _const_spec(bias.shape),
        ],
        out_specs=pl.BlockSpec((tb, hq_w), lambda b, i: (b * nsteps + i, 0)),
        out_shape=jax.ShapeDtypeStruct((t, hq_w), BF16),
        compiler_params=_params(("parallel", "parallel")),
        name="swa_attn",
    )(sink, qkv, qkv, qkv, qkv, qkv, qkv, qkv, bias)


def _out_proj_kernel(o_ref, w_ref, g_ref, x_ref, y_ref):
    y_ref[...] = x_ref[...] + _rms(_dot(o_ref[...], w_ref[...]), g_ref[...])


def _out_proj_residual(o, w, g, x, tm=512):
    t, d = x.shape
    return pl.pallas_call(
        _out_proj_kernel,
        grid=(t // tm,),
        in_specs=[
            pl.BlockSpec((tm, o.shape[1]), lambda i: (i, 0)),
            _const_spec(w.shape),
            _const_spec((1, d)),
            pl.BlockSpec((tm, d), lambda i: (i, 0)),
        ],
        out_specs=pl.BlockSpec((tm, d), lambda i: (i, 0)),
        out_shape=jax.ShapeDtypeStruct((t, d), F32),
        compiler_params=_params(("parallel",)),
        name="out_proj_residual",
    )(o, w, g, x)


def _mlp_kernel(x_ref, gin_ref, wup_ref, wdn_ref, gout_ref, y_ref, h_sc, acc_sc):
    j = pl.program_id(1)

    @pl.when(j == 0)
    def _():
        h_sc[...] = _rms(x_ref[...], gin_ref[...]).astype(BF16)

    u = jnp.maximum(_dot(h_sc[...], wup_ref[...]), 0.0)
    d = _dot((u * u).astype(BF16), wdn_ref[...])

    @pl.when(j == 0)
    def _():
        acc_sc[...] = d

    @pl.when(j > 0)
    def _():
        acc_sc[...] += d

    @pl.when(j == pl.num_programs(1) - 1)
    def _():
        y_ref[...] = x_ref[...] + _rms(acc_sc[...], gout_ref[...])


def _mlp(x, gin, wup, wdn, gout, tm=512, tf=1024):
    t, d = x.shape
    f = wup.shape[1]
    return pl.pallas_call(
        _mlp_kernel,
        grid=(t // tm, f // tf),
        in_specs=[
            pl.BlockSpec((tm, d), lambda i, j: (i, 0)),
            _const_spec((1, d)),
            pl.BlockSpec((d, tf), lambda i, j: (0, j)),
            pl.BlockSpec((tf, d), lambda i, j: (j, 0)),
            _const_spec((1, d)),
        ],
        out_specs=pl.BlockSpec((tm, d), lambda i, j: (i, 0)),
        out_shape=jax.ShapeDtypeStruct((t, d), F32),
        scratch_shapes=[pltpu.VMEM((tm, d), BF16), pltpu.VMEM((tm, d), F32)],
        compiler_params=_params(("parallel", "arbitrary")),
        name="mlp",
    )(x, gin, wup, wdn, gout)


def _ple_kernel(x_ref, p_ref, wpu_ref, pn_ref, wg_ref, y_ref):
    x = x_ref[...]
    e = _rms(_dot(p_ref[...].astype(BF16), wpu_ref[...]), pn_ref[...])
    z = _dot(x.astype(BF16), wg_ref[...])
    y_ref[...] = x + e / (1.0 + jnp.exp(-z))


def _ple(x, p, wpu, pn, wg, tm=512):
    t, d = x.shape
    return pl.pallas_call(
        _ple_kernel,
        grid=(t // tm,),
        in_specs=[
            pl.BlockSpec((tm, d), lambda i: (i, 0)),
            pl.BlockSpec((tm, p.shape[1]), lambda i: (i, 0)),
            _const_spec(wpu.shape),
            _const_spec((1, d)),
            _const_spec(wg.shape),
        ],
        out_specs=pl.BlockSpec((tm, d), lambda i: (i, 0)),
        out_shape=jax.ShapeDtypeStruct((t, d), F32),
        compiler_params=_params(("parallel",)),
        name="ple",
    )(x, p, wpu, pn, wg)


def _pad_cols(w, width):
    return jnp.pad(w, ((0, 0), (0, width - w.shape[1])))


def _swap_halves(w):
    half = w.shape[1] // 2
    return jnp.concatenate([w[:, half:], w[:, :half]], axis=1)


def _prep_mla_weights(w_down, w_uq):
    base = Q_LORA + KV_LORA
    w_kr = w_down[:, base:]
    wd = jnp.concatenate([w_down[:, :base], _pad_cols(w_kr, LANES), _pad_cols(_swap_halves(w_kr), LANES)], axis=1)
    wq = w_uq.reshape(Q_LORA, MLA_HEADS, NOPE_DIM + ROPE_DIM)
    nope = wq[:, :, :NOPE_DIM].reshape(Q_LORA, MLA_HEADS * NOPE_DIM)
    rope = wq[:, :, NOPE_DIM:]
    half = ROPE_DIM // 2
    swapped = jnp.concatenate([rope[:, :, half:], rope[:, :, :half]], axis=2)
    pad = ((0, 0), (0, 0), (0, LANES - ROPE_DIM))
    rope = jnp.pad(rope, pad).reshape(Q_LORA, MLA_HEADS * LANES)
    swapped = jnp.pad(swapped, pad).reshape(Q_LORA, MLA_HEADS * LANES)
    return wd.astype(BF16), jnp.concatenate([nope, rope, swapped], axis=1).astype(BF16)


def _rope_tables(seq):
    half = ROPE_DIM // 2
    inv = 1.0 / (ROPE_THETA ** (jnp.arange(half, dtype=F32) / half))
    ang = jnp.arange(seq).astype(F32)[:, None] * inv[None, :]
    cos = jnp.cos(ang)
    sin = jnp.sin(ang)
    return jnp.tile(cos, (1, LANES // half)), jnp.tile(jnp.concatenate([-sin, sin], axis=1), (1, LANES // ROPE_DIM))


def _trunk(x, p, batch, seq, w):
    cos, sin = _rope_tables(seq)
    for i in range(DEPTH):
        g = w["norm_gains"][i]
        j = i // N_MIXERS
        if i % N_MIXERS == 0:
            q, k, v = _mla_proj(x, g[0:1], w["mla_w_down"][j], w["mla_q_norm"][j:j + 1], w["mla_kv_norm"][j:j + 1],
                                w["mla_w_uq"][j], w["mla_w_ukv"][j], cos, sin, seq)
            o = _mla_attn(q, k, v, batch, seq)
            x = _out_proj_residual(o, w["mla_w_o"][j], g[1:2], x)
        else:
            qkv = _norm_matmul(x, g[0:1], w["swa_w_qkv"][j])
            o = _swa_attn(qkv, w["swa_sink"][j], w["bias_table"], batch, seq)
            x = _out_proj_residual(o, w["swa_w_o"][j], g[1:2], x)
        x = _mlp(x, g[2:3], w["mlp_w_up"][i], w["mlp_w_down"][i], g[3:4])
        x = _ple(x, p[i], w["ple_w_up"][i], w["ple_norm"][i:i + 1], w["ple_w_gate"][i])
    return x


def _prep_weights(norm_gains, mla_w_down, mla_q_norm, mla_kv_norm, mla_w_uq, mla_w_ukv, mla_w_o, swa_w_qkv,
                  swa_sink, swa_w_o, rel_bias, mlp_w_up, mlp_w_down, ple_w_up, ple_w_gate, ple_norm):
    wd, wq = zip(*[_prep_mla_weights(mla_w_down[j], mla_w_uq[j]) for j in range(mla_w_down.shape[0])])
    return dict(
        norm_gains=norm_gains, mla_w_down=wd, mla_q_norm=mla_q_norm, mla_kv_norm=mla_kv_norm, mla_w_uq=wq,
        mla_w_ukv=mla_w_ukv.astype(BF16), mla_w_o=mla_w_o.astype(BF16), swa_w_qkv=swa_w_qkv.astype(BF16),
        swa_sink=swa_sink, swa_w_o=swa_w_o.astype(BF16), bias_table=_bias_table(rel_bias),
        mlp_w_up=mlp_w_up.astype(BF16), mlp_w_down=mlp_w_down.astype(BF16), ple_w_up=ple_w_up.astype(BF16),
        ple_w_gate=ple_w_gate.astype(BF16), ple_norm=ple_norm)


def kernel(x_prompt, x_sample, p_prompt, p_sample, norm_gains, mla_w_down, mla_q_norm, mla_kv_norm, mla_w_uq,
           mla_w_ukv, mla_w_o, swa_w_qkv, swa_sink, swa_w_o, rel_bias, mlp_w_up, mlp_w_down, ple_w_up, ple_w_gate,
           ple_norm):
    w = _prep_weights(norm_gains, mla_w_down, mla_q_norm, mla_kv_norm, mla_w_uq, mla_w_ukv, mla_w_o, swa_w_qkv,
                      swa_sink, swa_w_o, rel_bias, mlp_w_up, mlp_w_down, ple_w_up, ple_w_gate, ple_norm)
    outs = []
    for x, p in ((x_prompt, p_prompt), (x_sample, p_sample)):
        b, s, d = x.shape
        y = _trunk(x.reshape(b * s, d), p.reshape(DEPTH, b * s, p.shape[-1]), b, s, w)
        outs.append(y.reshape(b, s, d))
    return tuple(outs)
```

```python
import functools

import numpy as np
import jax
import jax.numpy as jnp
from jax import lax
from jax.experimental import pallas as pl
from jax.experimental.pallas import tpu as pltpu

D_MODEL = 2048
DEPTH = 4
N_MIXERS = 2
MLA_HEADS = 16
Q_LORA = 512
KV_LORA = 512
NOPE_DIM = 128
ROPE_DIM = 64
V_DIM = 128
ROPE_THETA = 10000.0
SWA_Q_HEADS = 16
SWA_KV_HEADS = 4
SWA_GROUP = SWA_Q_HEADS // SWA_KV_HEADS
SWA_HEAD_DIM = 128
WINDOW = 128
BLOCK = 128
N_BUCKETS = 32
MAX_DISTANCE = 128
D_FF = 4 * D_MODEL
PLE_DIM = 256
EPS = 1e-6
NEG_INF = -1e30

LANES = 128
QK_PAD = 2 * LANES
VMEM_LIMIT = 56 * 1024 * 1024

F32 = jnp.float32
BF16 = jnp.bfloat16


def _rms(x, g):
    return x * lax.rsqrt(jnp.mean(x * x, axis=-1, keepdims=True) + EPS) * g


def _dot(a, b):
    return jnp.dot(a, b, preferred_element_type=F32)


def _dot_nt(a, b):
    return lax.dot_general(a, b, (((1,), (1,)), ((), ())), preferred_element_type=F32)


def _const_spec(shape):
    nd = len(shape)
    return pl.BlockSpec(shape, lambda *_: (0,) * nd)


def _params(sem):
    return pltpu.CompilerParams(dimension_semantics=sem, vmem_limit_bytes=VMEM_LIMIT)


def _norm_matmul_kernel(x_ref, g_ref, w_ref, o_ref, *, tn):
    h = _rms(x_ref[...], g_ref[...]).astype(BF16)
    for c in range(0, w_ref.shape[1], tn):
        o_ref[:, c:c + tn] = _dot(h, w_ref[:, c:c + tn]).astype(o_ref.dtype)


def _norm_matmul(x, g, w, tm=512, tn=1024):
    t, d = x.shape
    n = w.shape[1]
    return pl.pallas_call(
        functools.partial(_norm_matmul_kernel, tn=tn),
        grid=(t // tm,),
        in_specs=[pl.BlockSpec((tm, d), lambda i: (i, 0)), _const_spec((1, d)), _const_spec((d, n))],
        out_specs=pl.BlockSpec((tm, n), lambda i: (i, 0)),
        out_shape=jax.ShapeDtypeStruct((t, n), BF16),
        compiler_params=_params(("parallel",)),
        name="norm_matmul",
    )(x, g, w)


def _mla_proj_kernel(x_ref, g_ref, wd_ref, qn_ref, kvn_ref, wuq_ref, wukv_ref, cos_ref, sin_ref,
                     q_ref, k_ref, v_ref, *, scale):
    h = _rms(x_ref[...], g_ref[...]).astype(BF16)
    lat = _dot(h, wd_ref[...])
    cq = _rms(lat[:, :Q_LORA], qn_ref[...]).astype(BF16)
    ckv = _rms(lat[:, Q_LORA:Q_LORA + KV_LORA], kvn_ref[...]).astype(BF16)
    cos = cos_ref[...]
    sin = sin_ref[...]
    base = Q_LORA + KV_LORA
    kr = (lat[:, base:base + LANES] * cos + lat[:, base + LANES:base + 2 * LANES] * sin).astype(BF16)
    hn = MLA_HEADS * NOPE_DIM
    for hd in range(MLA_HEADS):
        lo = hd * LANES
        qn = _dot(cq, wuq_ref[:, lo:lo + LANES])
        qr = _dot(cq, wuq_ref[:, hn + lo:hn + lo + LANES])
        qs = _dot(cq, wuq_ref[:, 2 * hn + lo:2 * hn + lo + LANES])
        q_ref[hd, :LANES, :] = (qn * scale).T.astype(BF16)
        q_ref[hd, LANES:, :] = ((qr * cos + qs * sin) * scale).T.astype(BF16)
        kv = _dot(ckv, wukv_ref[:, 2 * lo:2 * lo + 2 * LANES])
        k_ref[hd, :, :LANES] = kv[:, :LANES].astype(BF16)
        k_ref[hd, :, LANES:] = kr
        v_ref[hd] = kv[:, LANES:].T.astype(BF16)


def _mla_proj(x, g, wd, qn, kvn, wuq, wukv, cos, sin, seq, tm=256):
    t, d = x.shape
    nblk = seq // tm
    scale = float((NOPE_DIM + ROPE_DIM) ** -0.5 * np.log2(np.e))
    heads = MLA_HEADS
    return pl.pallas_call(
        functools.partial(_mla_proj_kernel, scale=scale),
        grid=(t // tm,),
        in_specs=[
            pl.BlockSpec((tm, d), lambda i: (i, 0)),
            _const_spec((1, d)),
            _const_spec(wd.shape),
            _const_spec((1, Q_LORA)),
            _const_spec((1, KV_LORA)),
            _const_spec(wuq.shape),
            _const_spec(wukv.shape),
            pl.BlockSpec((tm, LANES), lambda i: (i % nblk, 0)),
            pl.BlockSpec((tm, LANES), lambda i: (i % nblk, 0)),
        ],
        out_specs=[
            pl.BlockSpec((heads, QK_PAD, tm), lambda i: (0, 0, i)),
            pl.BlockSpec((heads, tm, QK_PAD), lambda i: (0, i, 0)),
            pl.BlockSpec((heads, V_DIM, tm), lambda i: (0, 0, i)),
        ],
        out_shape=[
            jax.ShapeDtypeStruct((heads, QK_PAD, t), BF16),
            jax.ShapeDtypeStruct((heads, t, QK_PAD), BF16),
            jax.ShapeDtypeStruct((heads, V_DIM, t), BF16),
        ],
        compiler_params=_params(("parallel",)),
        name="mla_proj",
    )(x, g, wd, qn, kvn, wuq, wukv, cos, sin)


def _mla_attn_kernel(qt_ref, k_ref, vt_ref, o_ref, sa_sc, sb_sc, acc_sc, *, tkc):
    seq = k_ref.shape[0]
    npairs = seq // (2 * tkc)
    qt = qt_ref[...]
    tq = qt.shape[1]

    def scores(c, s_sc):
        off = pl.multiple_of(c * tkc, tkc)
        s = _dot(k_ref[pl.ds(off, tkc), :], qt)
        s_sc[...] = s
        return jnp.max(s, axis=0, keepdims=True)

    def accumulate(c, s_sc, m_run, l_run, m_chunk):
        off = pl.multiple_of(c * tkc, tkc)
        m_new = jnp.maximum(m_run, m_chunk)
        alpha = jnp.exp2(m_run - m_new)
        p = jnp.exp2(s_sc[...] - m_new)
        l_new = alpha * l_run + jnp.sum(p, axis=0, keepdims=True)
        pv = _dot(vt_ref[:, pl.ds(off, tkc)], p.astype(BF16))
        acc_sc[...] = alpha * acc_sc[...] + pv
        return m_new, l_new

    def pair(j, carry, last):
        m_run, l_run, m_a = carry
        m_b = scores(2 * j + 1, sb_sc)
        m_run, l_run = accumulate(2 * j, sa_sc, m_run, l_run, m_a)
        if not last:
            m_a = scores(2 * j + 2, sa_sc)
        m_run, l_run = accumulate(2 * j + 1, sb_sc, m_run, l_run, m_b)
        return m_run, l_run, m_a

    acc_sc[...] = jnp.zeros(acc_sc.shape, F32)
    init = (jnp.full((1, tq), -jnp.inf, F32), jnp.zeros((1, tq), F32), scores(0, sa_sc))
    carry = lax.fori_loop(0, npairs - 1, functools.partial(pair, last=False), init)
    _, l_run, _ = pair(npairs - 1, carry, last=True)
    o_ref[...] = (acc_sc[...] * (1.0 / l_run)).T.astype(o_ref.dtype)


def _mla_attn(qt, k, vt, batch, seq, tq=512, tkc=1024):
    heads, _, t = qt.shape
    nq = seq // tq
    return pl.pallas_call(
        functools.partial(_mla_attn_kernel, tkc=tkc),
        grid=(batch, heads, nq),
        in_specs=[
            pl.BlockSpec((None, QK_PAD, tq), lambda b, h, i: (h, 0, b * nq + i)),
            pl.BlockSpec((None, seq, QK_PAD), lambda b, h, i: (h, b, 0)),
            pl.BlockSpec((None, V_DIM, seq), lambda b, h, i: (h, 0, b)),
        ],
        out_specs=pl.BlockSpec((tq, V_DIM), lambda b, h, i: (b * nq + i, h)),
        out_shape=jax.ShapeDtypeStruct((t, heads * V_DIM), BF16),
        scratch_shapes=[
            pltpu.VMEM((tkc, tq), F32),
            pltpu.VMEM((tkc, tq), F32),
            pltpu.VMEM((V_DIM, tq), F32),
        ],
        compiler_params=_params(("parallel", "parallel", "arbitrary")),
        name="mla_attn",
    )(qt, k, vt)


def _t5_bucket(rel):
    nb = N_BUCKETS // 2
    max_exact = nb // 2
    ret = (rel > 0).astype(np.int32) * nb
    n = np.abs(rel)
    large = max_exact + (np.log(np.maximum(n, 1).astype(np.float32) / max_exact)
                         / np.log(MAX_DISTANCE / max_exact) * (nb - max_exact)).astype(np.int32)
    large = np.minimum(large, nb - 1)
    return (ret + np.where(n < max_exact, n, large)).astype(np.int32)


def _bias_table_kernel(rb_ref, bucket_ref, o_ref):
    hd = pl.program_id(0)
    bucket = bucket_ref[...]
    acc = jnp.zeros(bucket.shape, F32)
    for b in range(N_BUCKETS):
        acc = jnp.where(bucket == b, rb_ref[b, hd], acc)
    o_ref[...] = acc


def _bias_table(rel_bias):
    qi = np.arange(BLOCK)[:, None]
    si = np.arange(3 * BLOCK)[None, :]
    bucket = jnp.asarray(_t5_bucket(si - BLOCK - qi))
    return pl.pallas_call(
        _bias_table_kernel,
        grid=(SWA_Q_HEADS,),
        in_specs=[pl.BlockSpec(memory_space=pltpu.SMEM), _const_spec((BLOCK, 3 * BLOCK))],
        out_specs=pl.BlockSpec((None, BLOCK, 3 * BLOCK), lambda h: (h, 0, 0)),
        out_shape=jax.ShapeDtypeStruct((SWA_Q_HEADS, BLOCK, 3 * BLOCK), F32),
        name="t5_bias_table",
    )(rel_bias, bucket)


def _swa_attn_kernel(sink_ref, q_ref, kp_ref, kc_ref, kn_ref, vp_ref, vc_ref, vn_ref, bias_ref, o_ref,
                     *, nqb, nblocks):
    i = pl.program_id(1)
    dh = SWA_HEAD_DIM
    scale = float(dh ** -0.5)
    kband = jnp.concatenate([kp_ref[...], kc_ref[...], kn_ref[...]], axis=0)
    vband = jnp.concatenate([vp_ref[...], vc_ref[...], vn_ref[...]], axis=0)
    qi = lax.broadcasted_iota(jnp.int32, (BLOCK, 3 * BLOCK), 0)
    si = lax.broadcasted_iota(jnp.int32, (BLOCK, 3 * BLOCK), 1)
    rel = si - BLOCK - qi
    in_window = jnp.abs(rel) <= WINDOW
    for j in range(nqb):
        blk = i * nqb + j
        lo_ok = jnp.logical_or(si >= BLOCK, blk > 0)
        hi_ok = jnp.logical_or(si < 2 * BLOCK, blk < nblocks - 1)
        mask = in_window & lo_ok & hi_ok
        r0 = j * BLOCK
        for kh in range(SWA_KV_HEADS):
            kb = kband[r0:r0 + 3 * BLOCK, kh * dh:(kh + 1) * dh]
            vb = vband[r0:r0 + 3 * BLOCK, kh * dh:(kh + 1) * dh]
            for g in range(SWA_GROUP):
                hq = kh * SWA_GROUP + g
                qh = q_ref[r0:r0 + BLOCK, hq * dh:(hq + 1) * dh]
                s = _dot_nt(qh, kb) * scale
                s = jnp.where(mask, s + bias_ref[hq], NEG_INF)
                sk = sink_ref[hq]
                m = jnp.maximum(jnp.max(s, axis=-1, keepdims=True), sk)
                e = jnp.exp(s - m)
                p = e / (jnp.sum(e, axis=-1, keepdims=True) + jnp.exp(sk - m))
                o_ref[r0:r0 + BLOCK, hq * dh:(hq + 1) * dh] = _dot(p.astype(BF16), vb).astype(o_ref.dtype)


def _swa_attn(qkv, sink, bias, batch, seq, tb=256):
    t = qkv.shape[0]
    nqb = tb // BLOCK
    nsteps = seq // tb
    nblocks = seq // BLOCK
    hq_w = SWA_Q_HEADS * SWA_HEAD_DIM
    kv_w = SWA_KV_HEADS * SWA_HEAD_DIM
    kcol = hq_w // kv_w
    vcol = kcol + 1

    def prev_row(b, i):
        return b * nblocks + jnp.maximum(i * nqb - 1, 0)

    def next_row(b, i):
        return b * nblocks + jnp.minimum((i + 1) * nqb, nblocks - 1)

    return pl.pallas_call(
        functools.partial(_swa_attn_kernel, nqb=nqb, nblocks=nblocks),
        grid=(batch, nsteps),
        in_specs=[
            pl.BlockSpec(memory_space=pltpu.SMEM),
            pl.BlockSpec((tb, hq_w), lambda b, i: (b * nsteps + i, 0)),
            pl.BlockSpec((BLOCK, kv_w), lambda b, i: (prev_row(b, i), kcol)),
            pl.BlockSpec((tb, kv_w), lambda b, i: (b * nsteps + i, kcol)),
            pl.BlockSpec((BLOCK, kv_w), lambda b, i: (next_row(b, i), kcol)),
            pl.BlockSpec((BLOCK, kv_w), lambda b, i: (prev_row(b, i), vcol)),
            pl.BlockSpec((tb, kv_w), lambda b, i: (b * nsteps + i, vcol)),
            pl.BlockSpec((BLOCK, kv_w), lambda b, i: (next_row(b, i), vcol)),
            _const_spec(bias.shape),
        ],
        out_specs=pl.BlockSpec((tb, hq_w), lambda b, i: (b * nsteps + i, 0)),
        out_shape=jax.ShapeDtypeStruct((t, hq_w), BF16),
        compiler_params=_params(("parallel", "parallel")),
        name="swa_attn",
    )(sink, qkv, qkv, qkv, qkv, qkv, qkv, qkv, bias)


def _out_proj_kernel(o_ref, w_ref, g_ref, x_ref, y_ref):
    y_ref[...] = x_ref[...] + _rms(_dot(o_ref[...], w_ref[...]), g_ref[...])


def _out_proj_residual(o, w, g, x, tm=512):
    t, d = x.shape
    return pl.pallas_call(
        _out_proj_kernel,
        grid=(t // tm,),
        in_specs=[
            pl.BlockSpec((tm, o.shape[1]), lambda i: (i, 0)),
            _const_spec(w.shape),
            _const_spec((1, d)),
            pl.BlockSpec((tm, d), lambda i: (i, 0)),
        ],
        out_specs=pl.BlockSpec((tm, d), lambda i: (i, 0)),
        out_shape=jax.ShapeDtypeStruct((t, d), F32),
        compiler_params=_params(("parallel",)),
        name="out_proj_residual",
    )(o, w, g, x)


def _mlp_kernel(x_ref, gin_ref, wup_ref, wdn_ref, gout_ref, y_ref, h_sc, acc_sc):
    j = pl.program_id(1)

    @pl.when(j == 0)
    def _():
        h_sc[...] = _rms(x_ref[...], gin_ref[...]).astype(BF16)

    u = jnp.maximum(_dot(h_sc[...], wup_ref[...]), 0.0)
    d = _dot((u * u).astype(BF16), wdn_ref[...])

    @pl.when(j == 0)
    def _():
        acc_sc[...] = d

    @pl.when(j > 0)
    def _():
        acc_sc[...] += d

    @pl.when(j == pl.num_programs(1) - 1)
    def _():
        y_ref[...] = x_ref[...] + _rms(acc_sc[...], gout_ref[...])


def _mlp(x, gin, wup, wdn, gout, tm=512, tf=1024):
    t, d = x.shape
    f = wup.shape[1]
    return pl.pallas_call(
        _mlp_kernel,
        grid=(t // tm, f // tf),
        in_specs=[
            pl.BlockSpec((tm, d), lambda i, j: (i, 0)),
            _const_spec((1, d)),
            pl.BlockSpec((d, tf), lambda i, j: (0, j)),
            pl.BlockSpec((tf, d), lambda i, j: (j, 0)),
            _const_spec((1, d)),
        ],
        out_specs=pl.BlockSpec((tm, d), lambda i, j: (i, 0)),
        out_shape=jax.ShapeDtypeStruct((t, d), F32),
        scratch_shapes=[pltpu.VMEM((tm, d), BF16), pltpu.VMEM((tm, d), F32)],
        compiler_params=_params(("parallel", "arbitrary")),
        name="mlp",
    )(x, gin, wup, wdn, gout)


def _ple_kernel(x_ref, p_ref, wpu_ref, pn_ref, wg_ref, y_ref):
    x = x_ref[...]
    e = _rms(_dot(p_ref[...].astype(BF16), wpu_ref[...]), pn_ref[...])
    z = _dot(x.astype(BF16), wg_ref[...])
    y_ref[...] = x + e / (1.0 + jnp.exp(-z))


def _ple(x, p, wpu, pn, wg, tm=512):
    t, d = x.shape
    return pl.pallas_call(
        _ple_kernel,
        grid=(t // tm,),
        in_specs=[
            pl.BlockSpec((tm, d), lambda i: (i, 0)),
            pl.BlockSpec((tm, p.shape[1]), lambda i: (i, 0)),
            _const_spec(wpu.shape),
            _const_spec((1, d)),
            _const_spec(wg.shape),
        ],
        out_specs=pl.BlockSpec((tm, d), lambda i: (i, 0)),
        out_shape=jax.ShapeDtypeStruct((t, d), F32),
        compiler_params=_params(("parallel",)),
        name="ple",
    )(x, p, wpu, pn, wg)


def _pad_cols(w, width):
    return jnp.pad(w, ((0, 0), (0, width - w.shape[1])))


def _swap_halves(w):
    half = w.shape[1] // 2
    return jnp.concatenate([w[:, half:], w[:, :half]], axis=1)


def _prep_mla_weights(w_down, w_uq):
    base = Q_LORA + KV_LORA
    w_kr = w_down[:, base:]
    wd = jnp.concatenate([w_down[:, :base], _pad_cols(w_kr, LANES), _pad_cols(_swap_halves(w_kr), LANES)], axis=1)
    wq = w_uq.reshape(Q_LORA, MLA_HEADS, NOPE_DIM + ROPE_DIM)
    nope = wq[:, :, :NOPE_DIM].reshape(Q_LORA, MLA_HEADS * NOPE_DIM)
    rope = wq[:, :, NOPE_DIM:]
    half = ROPE_DIM // 2
    swapped = jnp.concatenate([rope[:, :, half:], rope[:, :, :half]], axis=2)
    pad = ((0, 0), (0, 0), (0, LANES - ROPE_DIM))
    rope = jnp.pad(rope, pad).reshape(Q_LORA, MLA_HEADS * LANES)
    swapped = jnp.pad(swapped, pad).reshape(Q_LORA, MLA_HEADS * LANES)
    return wd.astype(BF16), jnp.concatenate([nope, rope, swapped], axis=1).astype(BF16)


def _rope_tables(seq):
    half = ROPE_DIM // 2
    inv = 1.0 / (ROPE_THETA ** (jnp.arange(half, dtype=F32) / half))
    ang = jnp.arange(seq).astype(F32)[:, None] * inv[None, :]
    cos = jnp.cos(ang)
    sin = jnp.sin(ang)
    return jnp.tile(cos, (1, LANES // half)), jnp.tile(jnp.concatenate([-sin, sin], axis=1), (1, LANES // ROPE_DIM))


def _trunk(x, p, batch, seq, w):
    cos, sin = _rope_tables(seq)
    for i in range(DEPTH):
        g = w["norm_gains"][i]
        j = i // N_MIXERS
        if i % N_MIXERS == 0:
            q, k, v = _mla_proj(x, g[0:1], w["mla_w_down"][j], w["mla_q_norm"][j:j + 1], w["mla_kv_norm"][j:j + 1],
                                w["mla_w_uq"][j], w["mla_w_ukv"][j], cos, sin, seq)
            o = _mla_attn(q, k, v, batch, seq)
            x = _out_proj_residual(o, w["mla_w_o"][j], g[1:2], x)
        else:
            qkv = _norm_matmul(x, g[0:1], w["swa_w_qkv"][j])
            o = _swa_attn(qkv, w["swa_sink"][j], w["bias_table"], batch, seq)
            x = _out_proj_residual(o, w["swa_w_o"][j], g[1:2], x)
        x = _mlp(x, g[2:3], w["mlp_w_up"][i], w["mlp_w_down"][i], g[3:4])
        x = _ple(x, p[i], w["ple_w_up"][i], w["ple_norm"][i:i + 1], w["ple_w_gate"][i])
    return x


def _prep_weights(norm_gains, mla_w_down, mla_q_norm, mla_kv_norm, mla_w_uq, mla_w_ukv, mla_w_o, swa_w_qkv,
                  swa_sink, swa_w_o, rel_bias, mlp_w_up, mlp_w_down, ple_w_up, ple_w_gate, ple_norm):
    wd, wq = zip(*[_prep_mla_weights(mla_w_down[j], mla_w_uq[j]) for j in range(mla_w_down.shape[0])])
    return dict(
        norm_gains=norm_gains, mla_w_down=wd, mla_q_norm=mla_q_norm, mla_kv_norm=mla_kv_norm, mla_w_uq=wq,
        mla_w_ukv=mla_w_ukv.astype(BF16), mla_w_o=mla_w_o.astype(BF16), swa_w_qkv=swa_w_qkv.astype(BF16),
        swa_sink=swa_sink, swa_w_o=swa_w_o.astype(BF16), bias_table=_bias_table(rel_bias),
        mlp_w_up=mlp_w_up.astype(BF16), mlp_w_down=mlp_w_down.astype(BF16), ple_w_up=ple_w_up.astype(BF16),
        ple_w_gate=ple_w_gate.astype(BF16), ple_norm=ple_norm)


def kernel(x_prompt, x_sample, p_prompt, p_sample, norm_gains, mla_w_down, mla_q_norm, mla_kv_norm, mla_w_uq,
           mla_w_ukv, mla_w_o, swa_w_qkv, swa_sink, swa_w_o, rel_bias, mlp_w_up, mlp_w_down, ple_w_up, ple_w_gate,
           ple_norm):
    w = _prep_weights(norm_gains, mla_w_down, mla_q_norm, mla_kv_norm, mla_w_uq, mla_w_ukv, mla_w_o, swa_w_qkv,
                      swa_sink, swa_w_o, rel_bias, mlp_w_up, mlp_w_down, ple_w_up, ple_w_gate, ple_norm)
    outs = []
    for x, p in ((x_prompt, p_prompt), (x_sample, p_sample)):
        b, s, d = x.shape
        y = _trunk(x.reshape(b * s, d), p.reshape(DEPTH, b * s, p.shape[-1]), b, s, w)
        outs.append(y.reshape(b, s, d))
    return tuple(outs)
```

```python
import functools

import numpy as np
import jax
import jax.numpy as jnp
from jax import lax
from jax.experimental import pallas as pl
from jax.experimental.pallas import tpu as pltpu

D_MODEL = 2048
DEPTH = 4
N_MIXERS = 2
MLA_HEADS = 16
Q_LORA = 512
KV_LORA = 512
NOPE_DIM = 128
ROPE_DIM = 64
V_DIM = 128
ROPE_THETA = 10000.0
SWA_Q_HEADS = 16
SWA_KV_HEADS = 4
SWA_GROUP = SWA_Q_HEADS // SWA_KV_HEADS
SWA_HEAD_DIM = 128
WINDOW = 128
BLOCK = 128
N_BUCKETS = 32
MAX_DISTANCE = 128
D_FF = 4 * D_MODEL
PLE_DIM = 256
EPS = 1e-6
NEG_INF = -1e30

LANES = 128
QK_PAD = 2 * LANES
VMEM_LIMIT = 56 * 1024 * 1024

F32 = jnp.float32
BF16 = jnp.bfloat16


def _rms(x, g):
    return x * lax.rsqrt(jnp.mean(x * x, axis=-1, keepdims=True) + EPS) * g


def _dot(a, b):
    return jnp.dot(a, b, preferred_element_type=F32)


def _dot_nt(a, b):
    return lax.dot_general(a, b, (((1,), (1,)), ((), ())), preferred_element_type=F32)


def _const_spec(shape):
    nd = len(shape)
    return pl.BlockSpec(shape, lambda *_: (0,) * nd)


def _params(sem):
    return pltpu.CompilerParams(dimension_semantics=sem, vmem_limit_bytes=VMEM_LIMIT)


def _norm_matmul_kernel(x_ref, g_ref, w_ref, o_ref, *, tn):
    h = _rms(x_ref[...], g_ref[...]).astype(BF16)
    for c in range(0, w_ref.shape[1], tn):
        o_ref[:, c:c + tn] = _dot(h, w_ref[:, c:c + tn]).astype(o_ref.dtype)


def _norm_matmul(x, g, w, tm=512, tn=1024):
    t, d = x.shape
    n = w.shape[1]
    return pl.pallas_call(
        functools.partial(_norm_matmul_kernel, tn=tn),
        grid=(t // tm,),
        in_specs=[pl.BlockSpec((tm, d), lambda i: (i, 0)), _const_spec((1, d)), _const_spec((d, n))],
        out_specs=pl.BlockSpec((tm, n), lambda i: (i, 0)),
        out_shape=jax.ShapeDtypeStruct((t, n), BF16),
        compiler_params=_params(("parallel",)),
        name="norm_matmul",
    )(x, g, w)


def _mla_proj_kernel(x_ref, g_ref, wd_ref, qn_ref, kvn_ref, wuq_ref, wukv_ref, cos_ref, sin_ref,
                     q_ref, k_ref, v_ref, *, scale):
    h = _rms(x_ref[...], g_ref[...]).astype(BF16)
    lat = _dot(h, wd_ref[...])
    cq = _rms(lat[:, :Q_LORA], qn_ref[...]).astype(BF16)
    ckv = _rms(lat[:, Q_LORA:Q_LORA + KV_LORA], kvn_ref[...]).astype(BF16)
    cos = cos_ref[...]
    sin = sin_ref[...]
    base = Q_LORA + KV_LORA
    kr = (lat[:, base:base + LANES] * cos + lat[:, base + LANES:base + 2 * LANES] * sin).astype(BF16)
    hn = MLA_HEADS * NOPE_DIM
    grp = 4
    cos_g = jnp.tile(cos, (1, grp))
    sin_g = jnp.tile(sin, (1, grp))
    for h0 in range(0, MLA_HEADS, grp):
        lo = h0 * LANES
        w = grp * LANES
        qn = _dot(cq, wuq_ref[:, lo:lo + w]) * scale
        qr = _dot(cq, wuq_ref[:, hn + lo:hn + lo + w])
        qs = _dot(cq, wuq_ref[:, 2 * hn + lo:2 * hn + lo + w])
        qrot = (qr * cos_g + qs * sin_g) * scale
        kv = _dot(ckv, wukv_ref[:, 2 * lo:2 * lo + 2 * w])
        for g in range(grp):
            hd = h0 + g
            c = g * LANES
            q_ref[hd, :LANES, :] = qn[:, c:c + LANES].T.astype(BF16)
            q_ref[hd, LANES:, :] = qrot[:, c:c + LANES].T.astype(BF16)
            k_ref[hd, :, :LANES] = kv[:, 2 * c:2 * c + LANES].astype(BF16)
            k_ref[hd, :, LANES:] = kr
            v_ref[hd] = kv[:, 2 * c + LANES:2 * c + 2 * LANES].T.astype(BF16)


def _mla_proj(x, g, wd, qn, kvn, wuq, wukv, cos, sin, seq, tm=256):
    t, d = x.shape
    nblk = seq // tm
    scale = float((NOPE_DIM + ROPE_DIM) ** -0.5 * np.log2(np.e))
    heads = MLA_HEADS
    return pl.pallas_call(
        functools.partial(_mla_proj_kernel, scale=scale),
        grid=(t // tm,),
        in_specs=[
            pl.BlockSpec((tm, d), lambda i: (i, 0)),
            _const_spec((1, d)),
            _const_spec(wd.shape),
            _const_spec((1, Q_LORA)),
            _const_spec((1, KV_LORA)),
            _const_spec(wuq.shape),
            _const_spec(wukv.shape),
            pl.BlockSpec((tm, LANES), lambda i: (i % nblk, 0)),
            pl.BlockSpec((tm, LANES), lambda i: (i % nblk, 0)),
        ],
        out_specs=[
            pl.BlockSpec((heads, QK_PAD, tm), lambda i: (0, 0, i)),
            pl.BlockSpec((heads, tm, QK_PAD), lambda i: (0, i, 0)),
            pl.BlockSpec((heads, V_DIM, tm), lambda i: (0, 0, i)),
        ],
        out_shape=[
            jax.ShapeDtypeStruct((heads, QK_PAD, t), BF16),
            jax.ShapeDtypeStruct((heads, t, QK_PAD), BF16),
            jax.ShapeDtypeStruct((heads, V_DIM, t), BF16),
        ],
        compiler_params=_params(("parallel",)),
        name="mla_proj",
    )(x, g, wd, qn, kvn, wuq, wukv, cos, sin)


def _mla_attn_kernel(qt_ref, k_ref, vt_ref, o_ref, sa_sc, sb_sc, acc_sc, *, tkc):
    seq = k_ref.shape[0]
    npairs = seq // (2 * tkc)
    qt = qt_ref[...]
    tq = qt.shape[1]

    def scores(c, s_sc):
        off = pl.multiple_of(c * tkc, tkc)
        s = _dot(k_ref[pl.ds(off, tkc), :], qt)
        s_sc[...] = s
        return jnp.max(s, axis=0, keepdims=True)

    def accumulate(c, s_sc, m_run, l_run, m_chunk):
        off = pl.multiple_of(c * tkc, tkc)
        m_new = jnp.maximum(m_run, m_chunk)
        alpha = jnp.exp2(m_run - m_new)
        p = jnp.exp2(s_sc[...] - m_new)
        l_new = alpha * l_run + jnp.sum(p, axis=0, keepdims=True)
        pv = _dot(vt_ref[:, pl.ds(off, tkc)], p.astype(BF16))
        acc_sc[...] = alpha * acc_sc[...] + pv
        return m_new, l_new

    def pair(j, carry, last):
        m_run, l_run, m_a = carry
        m_b = scores(2 * j + 1, sb_sc)
        m_run, l_run = accumulate(2 * j, sa_sc, m_run, l_run, m_a)
        if not last:
            m_a = scores(2 * j + 2, sa_sc)
        m_run, l_run = accumulate(2 * j + 1, sb_sc, m_run, l_run, m_b)
        return m_run, l_run, m_a

    acc_sc[...] = jnp.zeros(acc_sc.shape, F32)
    init = (jnp.full((1, tq), -jnp.inf, F32), jnp.zeros((1, tq), F32), scores(0, sa_sc))
    carry = lax.fori_loop(0, npairs - 1, functools.partial(pair, last=False), init)
    _, l_run, _ = pair(npairs - 1, carry, last=True)
    o_ref[...] = (acc_sc[...] * (1.0 / l_run)).T.astype(o_ref.dtype)


def _mla_attn(qt, k, vt, batch, seq, tq=1024, tkc=1024):
    heads, _, t = qt.shape
    nq = seq // tq
    return pl.pallas_call(
        functools.partial(_mla_attn_kernel, tkc=tkc),
        grid=(batch, heads, nq),
        in_specs=[
            pl.BlockSpec((None, QK_PAD, tq), lambda b, h, i: (h, 0, b * nq + i)),
            pl.BlockSpec((None, seq, QK_PAD), lambda b, h, i: (h, b, 0)),
            pl.BlockSpec((None, V_DIM, seq), lambda b, h, i: (h, 0, b)),
        ],
        out_specs=pl.BlockSpec((tq, V_DIM), lambda b, h, i: (b * nq + i, h)),
        out_shape=jax.ShapeDtypeStruct((t, heads * V_DIM), BF16),
        scratch_shapes=[
            pltpu.VMEM((tkc, tq), F32),
            pltpu.VMEM((tkc, tq), F32),
            pltpu.VMEM((V_DIM, tq), F32),
        ],
        compiler_params=_params(("parallel", "parallel", "arbitrary")),
        name="mla_attn",
    )(qt, k, vt)


def _t5_bucket(rel):
    nb = N_BUCKETS // 2
    max_exact = nb // 2
    ret = (rel > 0).astype(np.int32) * nb
    n = np.abs(rel)
    large = max_exact + (np.log(np.maximum(n, 1).astype(np.float32) / max_exact)
                         / np.log(MAX_DISTANCE / max_exact) * (nb - max_exact)).astype(np.int32)
    large = np.minimum(large, nb - 1)
    return (ret + np.where(n < max_exact, n, large)).astype(np.int32)


def _bias_table_kernel(rb_ref, bucket_ref, o_ref):
    hd = pl.program_id(0)
    bucket = bucket_ref[...]
    acc = jnp.zeros(bucket.shape, F32)
    for b in range(N_BUCKETS):
        acc = jnp.where(bucket == b, rb_ref[b, hd], acc)
    o_ref[...] = acc


def _bias_table(rel_bias):
    qi = np.arange(BLOCK)[:, None]
    si = np.arange(3 * BLOCK)[None, :]
    bucket = jnp.asarray(_t5_bucket(si - BLOCK - qi))
    return pl.pallas_call(
        _bias_table_kernel,
        grid=(SWA_Q_HEADS,),
        in_specs=[pl.BlockSpec(memory_space=pltpu.SMEM), _const_spec((BLOCK, 3 * BLOCK))],
        out_specs=pl.BlockSpec((None, BLOCK, 3 * BLOCK), lambda h: (h, 0, 0)),
        out_shape=jax.ShapeDtypeStruct((SWA_Q_HEADS, BLOCK, 3 * BLOCK), F32),
        name="t5_bias_table",
    )(rel_bias, bucket)


def _swa_attn_kernel(sink_ref, q_ref, kp_ref, kc_ref, kn_ref, vp_ref, vc_ref, vn_ref, bias_ref, o_ref,
                     *, nqb, nblocks):
    i = pl.program_id(1)
    dh = SWA_HEAD_DIM
    scale = float(dh ** -0.5)
    kband = jnp.concatenate([kp_ref[...], kc_ref[...], kn_ref[...]], axis=0)
    vband = jnp.concatenate([vp_ref[...], vc_ref[...], vn_ref[...]], axis=0)
    qi = lax.broadcasted_iota(jnp.int32, (BLOCK, 3 * BLOCK), 0)
    si = lax.broadcasted_iota(jnp.int32, (BLOCK, 3 * BLOCK), 1)
    rel = si - BLOCK - qi
    in_window = jnp.abs(rel) <= WINDOW
    for j in range(nqb):
        blk = i * nqb + j
        lo_ok = jnp.logical_or(si >= BLOCK, blk > 0)
        hi_ok = jnp.logical_or(si < 2 * BLOCK, blk < nblocks - 1)
        mask = in_window & lo_ok & hi_ok
        r0 = j * BLOCK
        for kh in range(SWA_KV_HEADS):
            kb = kband[r0:r0 + 3 * BLOCK, kh * dh:(kh + 1) * dh]
            vb = vband[r0:r0 + 3 * BLOCK, kh * dh:(kh + 1) * dh]
            heads = range(kh * SWA_GROUP, (kh + 1) * SWA_GROUP)
            qs = jnp.concatenate([q_ref[r0:r0 + BLOCK, hq * dh:(hq + 1) * dh] for hq in heads], axis=0)
            s_all = _dot_nt(qs, kb) * scale
            ps = []
            for g, hq in enumerate(heads):
                s = jnp.where(mask, s_all[g * BLOCK:(g + 1) * BLOCK] + bias_ref[hq], NEG_INF)
                sk = sink_ref[hq]
                m = jnp.maximum(jnp.max(s, axis=-1, keepdims=True), sk)
                e = jnp.exp(s - m)
                denom = jnp.sum(e, axis=-1, keepdims=True) + jnp.exp(sk - m)
                ps.append((e / denom).astype(BF16))
            o_all = _dot(jnp.concatenate(ps, axis=0), vb)
            for g, hq in enumerate(heads):
                o_ref[r0:r0 + BLOCK, hq * dh:(hq + 1) * dh] = o_all[g * BLOCK:(g + 1) * BLOCK].astype(o_ref.dtype)


def _swa_attn(qkv, sink, bias, batch, seq, tb=256):
    t = qkv.shape[0]
    nqb = tb // BLOCK
    nsteps = seq // tb
    nblocks = seq // BLOCK
    hq_w = SWA_Q_HEADS * SWA_HEAD_DIM
    kv_w = SWA_KV_HEADS * SWA_HEAD_DIM
    kcol = hq_w // kv_w
    vcol = kcol + 1

    def prev_row(b, i):
        return b * nblocks + jnp.maximum(i * nqb - 1, 0)

    def next_row(b, i):
        return b * nblocks + jnp.minimum((i + 1) * nqb, nblocks - 1)

    return pl.pallas_call(
        functools.partial(_swa_attn_kernel, nqb=nqb, nblocks=nblocks),
        grid=(batch, nsteps),
        in_specs=[
            pl.BlockSpec(memory_space=pltpu.SMEM),
            pl.BlockSpec((tb, hq_w), lambda b, i: (b * nsteps + i, 0)),
            pl.BlockSpec((BLOCK, kv_w), lambda b, i: (prev_row(b, i), kcol)),
            pl.BlockSpec((tb, kv_w), lambda b, i: (b * nsteps + i, kcol)),
            pl.BlockSpec((BLOCK, kv_w), lambda b, i: (next_row(b, i), kcol)),
            pl.BlockSpec((BLOCK, kv_w), lambda b, i: (prev_row(b, i), vcol)),
            pl.BlockSpec((tb, kv_w), lambda b, i: (b * nsteps + i, vcol)),
            pl.BlockSpec((BLOCK, kv_w), lambda b, i: (next_row(b, i), vcol)),
            _const_spec(bias.shape),
        ],
        out_specs=pl.BlockSpec((tb, hq_w), lambda b, i: (b * nsteps + i, 0)),
        out_shape=jax.ShapeDtypeStruct((t, hq_w), BF16),
        compiler_params=_params(("parallel", "parallel")),
        name="swa_attn",
    )(sink, qkv, qkv, qkv, qkv, qkv, qkv, qkv, bias)


def _out_proj_kernel(o_ref, w_ref, g_ref, x_ref, y_ref):
    y_ref[...] = x_ref[...] + _rms(_dot(o_ref[...], w_ref[...]), g_ref[...])


def _out_proj_residual(o, w, g, x, tm=512):
    t, d = x.shape
    return pl.pallas_call(
        _out_proj_kernel,
        grid=(t // tm,),
        in_specs=[
            pl.BlockSpec((tm, o.shape[1]), lambda i: (i, 0)),
            _const_spec(w.shape),
            _const_spec((1, d)),
            pl.BlockSpec((tm, d), lambda i: (i, 0)),
        ],
        out_specs=pl.BlockSpec((tm, d), lambda i: (i, 0)),
        out_shape=jax.ShapeDtypeStruct((t, d), F32),
        compiler_params=_params(("parallel",)),
        name="out_proj_residual",
    )(o, w, g, x)


def _mlp_kernel(x_ref, gin_ref, wup_ref, wdn_ref, gout_ref, y_ref, h_sc, acc_sc):
    j = pl.program_id(1)

    @pl.when(j == 0)
    def _():
        h_sc[...] = _rms(x_ref[...], gin_ref[...]).astype(BF16)
        acc_sc[...] = jnp.zeros(acc_sc.shape, F32)

    u = jnp.maximum(_dot(h_sc[...], wup_ref[...]), 0.0)
    acc_sc[...] += _dot((u * u).astype(BF16), wdn_ref[...])

    @pl.when(j == pl.num_programs(1) - 1)
    def _():
        y_ref[...] = x_ref[...] + _rms(acc_sc[...], gout_ref[...])


def _mlp(x, gin, wup, wdn, gout, tm=512, tf=1024):
    t, d = x.shape
    f = wup.shape[1]
    return pl.pallas_call(
        _mlp_kernel,
        grid=(t // tm, f // tf),
        in_specs=[
            pl.BlockSpec((tm, d), lambda i, j: (i, 0)),
            _const_spec((1, d)),
            pl.BlockSpec((d, tf), lambda i, j: (0, j)),
            pl.BlockSpec((tf, d), lambda i, j: (j, 0)),
            _const_spec((1, d)),
        ],
        out_specs=pl.BlockSpec((tm, d), lambda i, j: (i, 0)),
        out_shape=jax.ShapeDtypeStruct((t, d), F32),
        scratch_shapes=[pltpu.VMEM((tm, d), BF16), pltpu.VMEM((tm, d), F32)],
        compiler_params=_params(("parallel", "arbitrary")),
        name="mlp",
    )(x, gin, wup, wdn, gout)


def _ple_kernel(x_ref, p_ref, wpu_ref, pn_ref, wg_ref, y_ref):
    x = x_ref[...]
    e = _rms(_dot(p_ref[...].astype(BF16), wpu_ref[...]), pn_ref[...])
    z = _dot(x.astype(BF16), wg_ref[...])
    y_ref[...] = x + e / (1.0 + jnp.exp(-z))


def _ple(x, p, wpu, pn, wg, tm=512):
    t, d = x.shape
    return pl.pallas_call(
        _ple_kernel,
        grid=(t // tm,),
        in_specs=[
            pl.BlockSpec((tm, d), lambda i: (i, 0)),
            pl.BlockSpec((tm, p.shape[1]), lambda i: (i, 0)),
            _const_spec(wpu.shape),
            _const_spec((1, d)),
            _const_spec(wg.shape),
        ],
        out_specs=pl.BlockSpec((tm, d), lambda i: (i, 0)),
        out_shape=jax.ShapeDtypeStruct((t, d), F32),
        compiler_params=_params(("parallel",)),
        name="ple",
    )(x, p, wpu, pn, wg)


def _pad_cols(w, width):
    return jnp.pad(w, ((0, 0), (0, width - w.shape[1])))


def _swap_halves(w):
    half = w.shape[1] // 2
    return jnp.concatenate([w[:, half:], w[:, :half]], axis=1)


def _prep_mla_weights(w_down, w_uq):
    base = Q_LORA + KV_LORA
    w_kr = w_down[:, base:]
    wd = jnp.concatenate([w_down[:, :base], _pad_cols(w_kr, LANES), _pad_cols(_swap_halves(w_kr), LANES)], axis=1)
    wq = w_uq.reshape(Q_LORA, MLA_HEADS, NOPE_DIM + ROPE_DIM)
    nope = wq[:, :, :NOPE_DIM].reshape(Q_LORA, MLA_HEADS * NOPE_DIM)
    rope = wq[:, :, NOPE_DIM:]
    half = ROPE_DIM // 2
    swapped = jnp.concatenate([rope[:, :, half:], rope[:, :, :half]], axis=2)
    pad = ((0, 0), (0, 0), (0, LANES - ROPE_DIM))
    rope = jnp.pad(rope, pad).reshape(Q_LORA, MLA_HEADS * LANES)
    swapped = jnp.pad(swapped, pad).reshape(Q_LORA, MLA_HEADS * LANES)
    return wd.astype(BF16), jnp.concatenate([nope, rope, swapped], axis=1).astype(BF16)


def _rope_tables(seq):
    half = ROPE_DIM // 2
    inv = 1.0 / (ROPE_THETA ** (jnp.arange(half, dtype=F32) / half))
    ang = jnp.arange(seq).astype(F32)[:, None] * inv[None, :]
    cos = jnp.cos(ang)
    sin = jnp.sin(ang)
    return jnp.tile(cos, (1, LANES // half)), jnp.tile(jnp.concatenate([-sin, sin], axis=1), (1, LANES // ROPE_DIM))


def _trunk(x, p, batch, seq, w):
    cos, sin = _rope_tables(seq)
    for i in range(DEPTH):
        g = w["norm_gains"][i]
        j = i // N_MIXERS
        if i % N_MIXERS == 0:
            q, k, v = _mla_proj(x, g[0:1], w["mla_w_down"][j], w["mla_q_norm"][j:j + 1], w["mla_kv_norm"][j:j + 1],
                                w["mla_w_uq"][j], w["mla_w_ukv"][j], cos, sin, seq)
            o = _mla_attn(q, k, v, batch, seq)
            x = _out_proj_residual(o, w["mla_w_o"][j], g[1:2], x)
        else:
            qkv = _norm_matmul(x, g[0:1], w["swa_w_qkv"][j])
            o = _swa_attn(qkv, w["swa_sink"][j], w["bias_table"], batch, seq)
            x = _out_proj_residual(o, w["swa_w_o"][j], g[1:2], x)
        x = _mlp(x, g[2:3], w["mlp_w_up"][i], w["mlp_w_down"][i], g[3:4])
        x = _ple(x, p[i], w["ple_w_up"][i], w["ple_norm"][i:i + 1], w["ple_w_gate"][i])
    return x


def _per_layer_bf16(w):
    return [w[i].astype(BF16) for i in range(w.shape[0])]


def _prep_weights(norm_gains, mla_w_down, mla_q_norm, mla_kv_norm, mla_w_uq, mla_w_ukv, mla_w_o, swa_w_qkv,
                  swa_sink, swa_w_o, rel_bias, mlp_w_up, mlp_w_down, ple_w_up, ple_w_gate, ple_norm):
    wd, wq = zip(*[_prep_mla_weights(mla_w_down[j], mla_w_uq[j]) for j in range(mla_w_down.shape[0])])
    return dict(
        norm_gains=norm_gains, mla_w_down=wd, mla_q_norm=mla_q_norm, mla_kv_norm=mla_kv_norm, mla_w_uq=wq,
        mla_w_ukv=_per_layer_bf16(mla_w_ukv), mla_w_o=_per_layer_bf16(mla_w_o),
        swa_w_qkv=_per_layer_bf16(swa_w_qkv), swa_sink=swa_sink, swa_w_o=_per_layer_bf16(swa_w_o),
        bias_table=_bias_table(rel_bias), mlp_w_up=_per_layer_bf16(mlp_w_up), mlp_w_down=_per_layer_bf16(mlp_w_down),
        ple_w_up=_per_layer_bf16(ple_w_up), ple_w_gate=_per_layer_bf16(ple_w_gate), ple_norm=ple_norm)


def kernel(x_prompt, x_sample, p_prompt, p_sample, norm_gains, mla_w_down, mla_q_norm, mla_kv_norm, mla_w_uq,
           mla_w_ukv, mla_w_o, swa_w_qkv, swa_sink, swa_w_o, rel_bias, mlp_w_up, mlp_w_down, ple_w_up, ple_w_gate,
           ple_norm):
    w = _prep_weights(norm_gains, mla_w_down, mla_q_norm, mla_kv_norm, mla_w_uq, mla_w_ukv, mla_w_o, swa_w_qkv,
                      swa_sink, swa_w_o, rel_bias, mlp_w_up, mlp_w_down, ple_w_up, ple_w_gate, ple_norm)
    outs = []
    for x, p in ((x_prompt, p_prompt), (x_sample, p_sample)):
        b, s, d = x.shape
        y = _trunk(x.reshape(b * s, d), p.reshape(DEPTH, b * s, p.shape[-1]), b, s, w)
        outs.append(y.reshape(b, s, d))
    return tuple(outs)
```

```python
import functools

import numpy as np
import jax
import jax.numpy as jnp
from jax import lax
from jax.experimental import pallas as pl
from jax.experimental.pallas import tpu as pltpu

D_MODEL = 2048
DEPTH = 4
N_MIXERS = 2
MLA_HEADS = 16
Q_LORA = 512
KV_LORA = 512
NOPE_DIM = 128
ROPE_DIM = 64
V_DIM = 128
ROPE_THETA = 10000.0
SWA_Q_HEADS = 16
SWA_KV_HEADS = 4
SWA_GROUP = SWA_Q_HEADS // SWA_KV_HEADS
SWA_HEAD_DIM = 128
WINDOW = 128
BLOCK = 128
N_BUCKETS = 32
MAX_DISTANCE = 128
D_FF = 4 * D_MODEL
PLE_DIM = 256
EPS = 1e-6
NEG_INF = -1e30

LANES = 128
BF16_SUBLANES = 16
QK_PAD = 2 * LANES
ONES_COL = NOPE_DIM + ROPE_DIM
K2_COL = ONES_COL + 1
VT_ROWS = V_DIM + BF16_SUBLANES
BOUND_MARGIN = 1.02
L_MIN = 2.0 ** -64
VMEM_LIMIT = 56 * 1024 * 1024

F32 = jnp.float32
BF16 = jnp.bfloat16


def _rms(x, g):
    return x * lax.rsqrt(jnp.mean(x * x, axis=-1, keepdims=True) + EPS) * g


def _dot(a, b):
    return jnp.dot(a, b, preferred_element_type=F32)


def _dot_nt(a, b):
    return lax.dot_general(a, b, (((1,), (1,)), ((), ())), preferred_element_type=F32)


def _const_spec(shape):
    nd = len(shape)
    return pl.BlockSpec(shape, lambda *_: (0,) * nd)


def _layer_spec(w, layer):
    return pl.BlockSpec((None,) + w.shape[1:], lambda *_: (layer, 0, 0))


def _params(sem):
    return pltpu.CompilerParams(dimension_semantics=sem, vmem_limit_bytes=VMEM_LIMIT)


def _norm_matmul_kernel(x_ref, g_ref, w_ref, o_ref, *, tn):
    h = _rms(x_ref[...], g_ref[...]).astype(BF16)
    for c in range(0, w_ref.shape[1], tn):
        o_ref[:, c:c + tn] = _dot(h, w_ref[:, c:c + tn]).astype(o_ref.dtype)


def _norm_matmul(x, g, w, layer, tm=512, tn=1024):
    t, d = x.shape
    n = w.shape[-1]
    return pl.pallas_call(
        functools.partial(_norm_matmul_kernel, tn=tn),
        grid=(t // tm,),
        in_specs=[pl.BlockSpec((tm, d), lambda i: (i, 0)), _const_spec((1, d)), _layer_spec(w, layer)],
        out_specs=pl.BlockSpec((tm, n), lambda i: (i, 0)),
        out_shape=jax.ShapeDtypeStruct((t, n), BF16),
        compiler_params=_params(("parallel",)),
        name="norm_matmul",
    )(x, g, w)


def _mla_proj_kernel(x_ref, g_ref, wd_ref, qn_ref, kvn_ref, wuq_ref, wukv_ref, cos_ref, sin_ref,
                     q_ref, k_ref, v_ref, *, scale):
    h = _rms(x_ref[...], g_ref[...]).astype(BF16)
    lat = _dot(h, wd_ref[...])
    cq = _rms(lat[:, :Q_LORA], qn_ref[...]).astype(BF16)
    ckv = _rms(lat[:, Q_LORA:Q_LORA + KV_LORA], kvn_ref[...]).astype(BF16)
    cos = cos_ref[...]
    sin = sin_ref[...]
    base = Q_LORA + KV_LORA
    kr = lat[:, base:base + LANES] * cos + lat[:, base + LANES:base + 2 * LANES] * sin
    kr2 = jnp.sum(kr * kr, axis=1, keepdims=True)
    lane = lax.broadcasted_iota(jnp.int32, kr.shape, 1)
    ones_rows = (lax.broadcasted_iota(jnp.int32, (VT_ROWS - V_DIM, kr.shape[0]), 0) == 0).astype(BF16)
    hn = MLA_HEADS * NOPE_DIM
    grp = 4
    cos_g = jnp.tile(cos, (1, grp))
    sin_g = jnp.tile(sin, (1, grp))
    for h0 in range(0, MLA_HEADS, grp):
        lo = h0 * LANES
        w = grp * LANES
        qn = _dot(cq, wuq_ref[:, lo:lo + w]) * scale
        qr = _dot(cq, wuq_ref[:, hn + lo:hn + lo + w])
        qs = _dot(cq, wuq_ref[:, 2 * hn + lo:2 * hn + lo + w])
        qrot = (qr * cos_g + qs * sin_g) * scale
        kv = _dot(ckv, wukv_ref[:, 2 * lo:2 * lo + 2 * w])
        for g in range(grp):
            hd = h0 + g
            c = g * LANES
            q_ref[hd, :LANES, :] = qn[:, c:c + LANES].T.astype(BF16)
            q_ref[hd, LANES:, :] = qrot[:, c:c + LANES].T.astype(BF16)
            kn = kv[:, 2 * c:2 * c + LANES]
            k_ref[hd, :, :LANES] = kn.astype(BF16)
            k2 = (jnp.sum(kn * kn, axis=1, keepdims=True) + kr2) * BOUND_MARGIN
            k_hi = jnp.where(lane == K2_COL - LANES, k2, kr)
            k_ref[hd, :, LANES:] = jnp.where(lane == ONES_COL - LANES, 1.0, k_hi).astype(BF16)
            v_ref[hd, :V_DIM, :] = kv[:, 2 * c + LANES:2 * c + 2 * LANES].T.astype(BF16)
            v_ref[hd, V_DIM:, :] = ones_rows


def _mla_proj(x, g, wd, qn, kvn, wuq, wukv, layer, cos, sin, seq, tm=256):
    t, d = x.shape
    nblk = seq // tm
    scale = float((NOPE_DIM + ROPE_DIM) ** -0.5 * np.log2(np.e))
    heads = MLA_HEADS
    return pl.pallas_call(
        functools.partial(_mla_proj_kernel, scale=scale),
        grid=(t // tm,),
        in_specs=[
            pl.BlockSpec((tm, d), lambda i: (i, 0)),
            _const_spec((1, d)),
            _const_spec(wd.shape),
            _const_spec((1, Q_LORA)),
            _const_spec((1, KV_LORA)),
            _const_spec(wuq.shape),
            _layer_spec(wukv, layer),
            pl.BlockSpec((tm, LANES), lambda i: (i % nblk, 0)),
            pl.BlockSpec((tm, LANES), lambda i: (i % nblk, 0)),
        ],
        out_specs=[
            pl.BlockSpec((heads, QK_PAD, tm), lambda i: (0, 0, i)),
            pl.BlockSpec((heads, tm, QK_PAD), lambda i: (0, i, 0)),
            pl.BlockSpec((heads, VT_ROWS, tm), lambda i: (0, 0, i)),
        ],
        out_shape=[
            jax.ShapeDtypeStruct((heads, QK_PAD, t), BF16),
            jax.ShapeDtypeStruct((heads, t, QK_PAD), BF16),
            jax.ShapeDtypeStruct((heads, VT_ROWS, t), BF16),
        ],
        compiler_params=_params(("parallel",)),
        name="mla_proj",
    )(x, g, wd, qn, kvn, wuq, wukv, cos, sin)


def _mla_attn_fixed_shift_kernel(qt_ref, k_ref, vt_ref, o_ref, l_ref, acc_sc, k2_sc, *, tkc, group):
    seq = k_ref.shape[0]
    qt = qt_ref[...]
    tq = qt.shape[1]

    @pl.when(pl.program_id(2) == 0)
    def _():
        colmax = jnp.max(k_ref[:, LANES:].astype(F32), axis=0, keepdims=True)
        lane = lax.broadcasted_iota(jnp.int32, colmax.shape, 1)
        k2max = jnp.max(jnp.where(lane == K2_COL - LANES, colmax, 0.0), axis=1, keepdims=True)
        k2_sc[...] = jnp.broadcast_to(k2max, k2_sc.shape)

    qf = qt.astype(F32)
    q2 = jnp.sum(qf * qf, axis=0, keepdims=True)
    shift = -(jnp.sqrt(q2 * jnp.tile(k2_sc[0:1, :], (1, tq // LANES))) * BOUND_MARGIN)
    row = lax.broadcasted_iota(jnp.int32, qt.shape, 0)
    q_aug = jnp.where(row == ONES_COL, shift.astype(BF16), qt)

    def chunk(c):
        off = pl.multiple_of(c * tkc, tkc)
        p = jnp.exp2(_dot(k_ref[pl.ds(off, tkc), :], q_aug)).astype(BF16)
        return _dot(vt_ref[:, pl.ds(off, tkc)], p)

    def body(j, carry):
        upd = chunk(group * j)
        for g in range(1, group):
            upd = upd + chunk(group * j + g)
        acc_sc[...] += upd
        return carry

    acc_sc[...] = jnp.zeros(acc_sc.shape, F32)
    lax.fori_loop(0, seq // (group * tkc), body, 0)
    l = acc_sc[V_DIM:V_DIM + 1, :]
    l_ref[...] = l
    o_ref[...] = (acc_sc[:V_DIM, :] * (1.0 / l)).T.astype(o_ref.dtype)


def _mla_attn_fixed_shift(qt, k, vt, batch, seq, tq=1024, tkc=1024, group=2):
    heads, _, t = qt.shape
    nq = seq // tq
    return pl.pallas_call(
        functools.partial(_mla_attn_fixed_shift_kernel, tkc=tkc, group=group),
        grid=(batch, heads, nq),
        in_specs=[
            pl.BlockSpec((None, QK_PAD, tq), lambda b, h, i: (h, 0, b * nq + i)),
            pl.BlockSpec((None, seq, QK_PAD), lambda b, h, i: (h, b, 0)),
            pl.BlockSpec((None, VT_ROWS, seq), lambda b, h, i: (h, 0, b)),
        ],
        out_specs=[
            pl.BlockSpec((tq, V_DIM), lambda b, h, i: (b * nq + i, h)),
            pl.BlockSpec((None, 1, tq), lambda b, h, i: (h, 0, b * nq + i)),
        ],
        out_shape=[jax.ShapeDtypeStruct((t, heads * V_DIM), BF16), jax.ShapeDtypeStruct((heads, 1, t), F32)],
        scratch_shapes=[pltpu.VMEM((VT_ROWS, tq), F32), pltpu.VMEM((8, LANES), F32)],
        compiler_params=_params(("parallel", "parallel", "arbitrary")),
        name="mla_attn_fixed_shift",
    )(qt, k, vt)


def _mla_attn_online_max_kernel(qt_ref, k_ref, vt_ref, o_ref, sa_sc, sb_sc, acc_sc, *, tkc):
    seq = k_ref.shape[0]
    npairs = seq // (2 * tkc)
    qt = qt_ref[...]
    tq = qt.shape[1]

    def scores(c, s_sc):
        off = pl.multiple_of(c * tkc, tkc)
        s = _dot(k_ref[pl.ds(off, tkc), :], qt)
        s_sc[...] = s
        return jnp.max(s, axis=0, keepdims=True)

    def accumulate(c, s_sc, m_run, l_run, m_chunk):
        off = pl.multiple_of(c * tkc, tkc)
        m_new = jnp.maximum(m_run, m_chunk)
        alpha = jnp.exp2(m_run - m_new)
        p = jnp.exp2(s_sc[...] - m_new)
        l_new = alpha * l_run + jnp.sum(p, axis=0, keepdims=True)
        pv = _dot(vt_ref[:, pl.ds(off, tkc)], p.astype(BF16))
        acc_sc[...] = alpha * acc_sc[...] + pv
        return m_new, l_new

    def pair(j, carry, last):
        m_run, l_run, m_a = carry
        m_b = scores(2 * j + 1, sb_sc)
        m_run, l_run = accumulate(2 * j, sa_sc, m_run, l_run, m_a)
        if not last:
            m_a = scores(2 * j + 2, sa_sc)
        m_run, l_run = accumulate(2 * j + 1, sb_sc, m_run, l_run, m_b)
        return m_run, l_run, m_a

    acc_sc[...] = jnp.zeros(acc_sc.shape, F32)
    init = (jnp.full((1, tq), -jnp.inf, F32), jnp.zeros((1, tq), F32), scores(0, sa_sc))
    carry = lax.fori_loop(0, npairs - 1, functools.partial(pair, last=False), init)
    _, l_run, _ = pair(npairs - 1, carry, last=True)
    o_ref[...] = (acc_sc[...] * (1.0 / l_run)).T.astype(o_ref.dtype)


def _mla_attn_online_max(qt, k, vt, batch, seq, tq=1024, tkc=1024):
    heads, _, t = qt.shape
    nq = seq // tq
    return pl.pallas_call(
        functools.partial(_mla_attn_online_max_kernel, tkc=tkc),
        grid=(batch, heads, nq),
        in_specs=[
            pl.BlockSpec((None, QK_PAD, tq), lambda b, h, i: (h, 0, b * nq + i)),
            pl.BlockSpec((None, seq, QK_PAD), lambda b, h, i: (h, b, 0)),
            pl.BlockSpec((None, V_DIM, seq), lambda b, h, i: (h, 0, b)),
        ],
        out_specs=pl.BlockSpec((tq, V_DIM), lambda b, h, i: (b * nq + i, h)),
        out_shape=jax.ShapeDtypeStruct((t, heads * V_DIM), BF16),
        scratch_shapes=[
            pltpu.VMEM((tkc, tq), F32),
            pltpu.VMEM((tkc, tq), F32),
            pltpu.VMEM((V_DIM, tq), F32),
        ],
        compiler_params=_params(("parallel", "parallel", "arbitrary")),
        name="mla_attn_online_max",
    )(qt, k, vt)


def _mla_attn(qt, k, vt, batch, seq):
    o, l = _mla_attn_fixed_shift(qt, k, vt, batch, seq)
    return lax.cond(jnp.min(l) >= L_MIN, lambda: o, lambda: _mla_attn_online_max(qt, k, vt, batch, seq))


def _t5_bucket(rel):
    nb = N_BUCKETS // 2
    max_exact = nb // 2
    ret = (rel > 0).astype(np.int32) * nb
    n = np.abs(rel)
    large = max_exact + (np.log(np.maximum(n, 1).astype(np.float32) / max_exact)
                         / np.log(MAX_DISTANCE / max_exact) * (nb - max_exact)).astype(np.int32)
    large = np.minimum(large, nb - 1)
    return (ret + np.where(n < max_exact, n, large)).astype(np.int32)


def _bias_table_kernel(rb_ref, bucket_ref, o_ref):
    hd = pl.program_id(0)
    bucket = bucket_ref[...]
    acc = jnp.zeros(bucket.shape, F32)
    for b in range(N_BUCKETS):
        acc = jnp.where(bucket == b, rb_ref[b, hd], acc)
    o_ref[...] = acc


def _bias_table(rel_bias):
    qi = np.arange(BLOCK)[:, None]
    si = np.arange(3 * BLOCK)[None, :]
    bucket = jnp.asarray(_t5_bucket(si - BLOCK - qi))
    return pl.pallas_call(
        _bias_table_kernel,
        grid=(SWA_Q_HEADS,),
        in_specs=[pl.BlockSpec(memory_space=pltpu.SMEM), _const_spec((BLOCK, 3 * BLOCK))],
        out_specs=pl.BlockSpec((None, BLOCK, 3 * BLOCK), lambda h: (h, 0, 0)),
        out_shape=jax.ShapeDtypeStruct((SWA_Q_HEADS, BLOCK, 3 * BLOCK), F32),
        name="t5_bias_table",
    )(rel_bias, bucket)


def _swa_attn_kernel(sink_ref, q_ref, kp_ref, kc_ref, kn_ref, vp_ref, vc_ref, vn_ref, bias_ref, o_ref,
                     *, nqb, nblocks):
    i = pl.program_id(1)
    dh = SWA_HEAD_DIM
    scale = float(dh ** -0.5)
    kband = jnp.concatenate([kp_ref[...], kc_ref[...], kn_ref[...]], axis=0)
    vband = jnp.concatenate([vp_ref[...], vc_ref[...], vn_ref[...]], axis=0)
    qi = lax.broadcasted_iota(jnp.int32, (BLOCK, 3 * BLOCK), 0)
    si = lax.broadcasted_iota(jnp.int32, (BLOCK, 3 * BLOCK), 1)
    rel = si - BLOCK - qi
    in_window = jnp.abs(rel) <= WINDOW
    for j in range(nqb):
        blk = i * nqb + j
        lo_ok = jnp.logical_or(si >= BLOCK, blk > 0)
        hi_ok = jnp.logical_or(si < 2 * BLOCK, blk < nblocks - 1)
        mask = in_window & lo_ok & hi_ok
        r0 = j * BLOCK
        for kh in range(SWA_KV_HEADS):
            kb = kband[r0:r0 + 3 * BLOCK, kh * dh:(kh + 1) * dh]
            vb = vband[r0:r0 + 3 * BLOCK, kh * dh:(kh + 1) * dh]
            heads = range(kh * SWA_GROUP, (kh + 1) * SWA_GROUP)
            qs = jnp.concatenate([q_ref[r0:r0 + BLOCK, hq * dh:(hq + 1) * dh] for hq in heads], axis=0)
            s_all = _dot_nt(qs, kb) * scale
            ps = []
            for g, hq in enumerate(heads):
                s = jnp.where(mask, s_all[g * BLOCK:(g + 1) * BLOCK] + bias_ref[hq], NEG_INF)
                sk = sink_ref[hq]
                m = jnp.maximum(jnp.max(s, axis=-1, keepdims=True), sk)
                e = jnp.exp(s - m)
                denom = jnp.sum(e, axis=-1, keepdims=True) + jnp.exp(sk - m)
                ps.append((e / denom).astype(BF16))
            o_all = _dot(jnp.concatenate(ps, axis=0), vb)
            for g, hq in enumerate(heads):
                o_ref[r0:r0 + BLOCK, hq * dh:(hq + 1) * dh] = o_all[g * BLOCK:(g + 1) * BLOCK].astype(o_ref.dtype)


def _swa_attn(qkv, sink, bias, batch, seq, tb=256):
    t = qkv.shape[0]
    nqb = tb // BLOCK
    nsteps = seq // tb
    nblocks = seq // BLOCK
    hq_w = SWA_Q_HEADS * SWA_HEAD_DIM
    kv_w = SWA_KV_HEADS * SWA_HEAD_DIM
    kcol = hq_w // kv_w
    vcol = kcol + 1

    def prev_row(b, i):
        return b * nblocks + jnp.maximum(i * nqb - 1, 0)

    def next_row(b, i):
        return b * nblocks + jnp.minimum((i + 1) * nqb, nblocks - 1)

    return pl.pallas_call(
        functools.partial(_swa_attn_kernel, nqb=nqb, nblocks=nblocks),
        grid=(batch, nsteps),
        in_specs=[
            pl.BlockSpec(memory_space=pltpu.SMEM),
            pl.BlockSpec((tb, hq_w), lambda b, i: (b * nsteps + i, 0)),
            pl.BlockSpec((BLOCK, kv_w), lambda b, i: (prev_row(b, i), kcol)),
            pl.BlockSpec((tb, kv_w), lambda b, i: (b * nsteps + i, kcol)),
            pl.BlockSpec((BLOCK, kv_w), lambda b, i: (next_row(b, i), kcol)),
            pl.BlockSpec((BLOCK, kv_w), lambda b, i: (prev_row(b, i), vcol)),
            pl.BlockSpec((tb, kv_w), lambda b, i: (b * nsteps + i, vcol)),
            pl.BlockSpec((BLOCK, kv_w), lambda b, i: (next_row(b, i), vcol)),
            _const_spec(bias.shape),
        ],
        out_specs=pl.BlockSpec((tb, hq_w), lambda b, i: (b * nsteps + i, 0)),
        out_shape=jax.ShapeDtypeStruct((t, hq_w), BF16),
        compiler_params=_params(("parallel", "parallel")),
        name="swa_attn",
    )(sink, qkv, qkv, qkv, qkv, qkv, qkv, qkv, bias)


def _out_proj_kernel(o_ref, w_ref, g_ref, x_ref, y_ref):
    y_ref[...] = x_ref[...] + _rms(_dot(o_ref[...], w_ref[...]), g_ref[...])


def _out_proj_residual(o, w, layer, g, x, tm=512):
    t, d = x.shape
    return pl.pallas_call(
        _out_proj_kernel,
        grid=(t // tm,),
        in_specs=[
            pl.BlockSpec((tm, o.shape[1]), lambda i: (i, 0)),
            _layer_spec(w, layer),
            _const_spec((1, d)),
            pl.BlockSpec((tm, d), lambda i: (i, 0)),
        ],
        out_specs=pl.BlockSpec((tm, d), lambda i: (i, 0)),
        out_shape=jax.ShapeDtypeStruct((t, d), F32),
        compiler_params=_params(("parallel",)),
        name="out_proj_residual",
    )(o, w, g, x)


def _mlp_kernel(x_ref, gin_ref, wup_ref, wdn_ref, gout_ref, y_ref, h_sc, acc_sc):
    j = pl.program_id(1)

    @pl.when(j == 0)
    def _():
        h_sc[...] = _rms(x_ref[...], gin_ref[...]).astype(BF16)
        acc_sc[...] = jnp.zeros(acc_sc.shape, F32)

    u = jnp.maximum(_dot(h_sc[...], wup_ref[...]), 0.0)
    acc_sc[...] += _dot((u * u).astype(BF16), wdn_ref[...])

    @pl.when(j == pl.num_programs(1) - 1)
    def _():
        y_ref[...] = x_ref[...] + _rms(acc_sc[...], gout_ref[...])


def _mlp(x, gin, wup, wdn, gout, layer, tm=512, tf=1024):
    t, d = x.shape
    f = wup.shape[-1]
    return pl.pallas_call(
        _mlp_kernel,
        grid=(t // tm, f // tf),
        in_specs=[
            pl.BlockSpec((tm, d), lambda i, j: (i, 0)),
            _const_spec((1, d)),
            pl.BlockSpec((None, d, tf), lambda i, j: (layer, 0, j)),
            pl.BlockSpec((None, tf, d), lambda i, j: (layer, j, 0)),
            _const_spec((1, d)),
        ],
        out_specs=pl.BlockSpec((tm, d), lambda i, j: (i, 0)),
        out_shape=jax.ShapeDtypeStruct((t, d), F32),
        scratch_shapes=[pltpu.VMEM((tm, d), BF16), pltpu.VMEM((tm, d), F32)],
        compiler_params=_params(("parallel", "arbitrary")),
        name="mlp",
    )(x, gin, wup, wdn, gout)


def _ple_kernel(x_ref, p_ref, wpu_ref, pn_ref, wg_ref, y_ref):
    x = x_ref[...]
    e = _rms(_dot(p_ref[...].astype(BF16), wpu_ref[...]), pn_ref[...])
    z = _dot(x.astype(BF16), wg_ref[...])
    y_ref[...] = x + e / (1.0 + jnp.exp(-z))


def _ple(x, p, wpu, pn, wg, layer, tm=512):
    t, d = x.shape
    return pl.pallas_call(
        _ple_kernel,
        grid=(t // tm,),
        in_specs=[
            pl.BlockSpec((tm, d), lambda i: (i, 0)),
            pl.BlockSpec((None, tm, p.shape[-1]), lambda i: (layer, i, 0)),
            _layer_spec(wpu, layer),
            _const_spec((1, d)),
            _layer_spec(wg, layer),
        ],
        out_specs=pl.BlockSpec((tm, d), lambda i: (i, 0)),
        out_shape=jax.ShapeDtypeStruct((t, d), F32),
        compiler_params=_params(("parallel",)),
        name="ple",
    )(x, p, wpu, pn, wg)


def _pad_cols(w, width):
    return jnp.pad(w, ((0, 0), (0, width - w.shape[1])))


def _swap_halves(w):
    half = w.shape[1] // 2
    return jnp.concatenate([w[:, half:], w[:, :half]], axis=1)


def _prep_mla_weights(w_down, w_uq):
    base = Q_LORA + KV_LORA
    w_kr = w_down[:, base:]
    wd = jnp.concatenate([w_down[:, :base], _pad_cols(w_kr, LANES), _pad_cols(_swap_halves(w_kr), LANES)], axis=1)
    wq = w_uq.reshape(Q_LORA, MLA_HEADS, NOPE_DIM + ROPE_DIM)
    nope = wq[:, :, :NOPE_DIM].reshape(Q_LORA, MLA_HEADS * NOPE_DIM)
    rope = wq[:, :, NOPE_DIM:]
    half = ROPE_DIM // 2
    swapped = jnp.concatenate([rope[:, :, half:], rope[:, :, :half]], axis=2)
    pad = ((0, 0), (0, 0), (0, LANES - ROPE_DIM))
    rope = jnp.pad(rope, pad).reshape(Q_LORA, MLA_HEADS * LANES)
    swapped = jnp.pad(swapped, pad).reshape(Q_LORA, MLA_HEADS * LANES)
    return wd.astype(BF16), jnp.concatenate([nope, rope, swapped], axis=1).astype(BF16)


def _rope_tables(seq):
    half = ROPE_DIM // 2
    inv = 1.0 / (ROPE_THETA ** (jnp.arange(half, dtype=F32) / half))
    ang = jnp.arange(seq).astype(F32)[:, None] * inv[None, :]
    cos = jnp.cos(ang)
    sin = jnp.sin(ang)
    return jnp.tile(cos, (1, LANES // half)), jnp.tile(jnp.concatenate([-sin, sin], axis=1), (1, LANES // ROPE_DIM))


def _trunk(x, p, batch, seq, w):
    cos, sin = _rope_tables(seq)
    for i in range(DEPTH):
        g = w["norm_gains"][i]
        j = i // N_MIXERS
        if i % N_MIXERS == 0:
            q, k, v = _mla_proj(x, g[0:1], w["mla_w_down"][j], w["mla_q_norm"][j:j + 1], w["mla_kv_norm"][j:j + 1],
                                w["mla_w_uq"][j], w["mla_w_ukv"], j, cos, sin, seq)
            o = _mla_attn(q, k, v, batch, seq)
            x = _out_proj_residual(o, w["mla_w_o"], j, g[1:2], x)
        else:
            qkv = _norm_matmul(x, g[0:1], w["swa_w_qkv"], j)
            o = _swa_attn(qkv, w["swa_sink"][j], w["bias_table"], batch, seq)
            x = _out_proj_residual(o, w["swa_w_o"], j, g[1:2], x)
        x = _mlp(x, g[2:3], w["mlp_w_up"], w["mlp_w_down"], g[3:4], i)
        x = _ple(x, p, w["ple_w_up"], w["ple_norm"][i:i + 1], w["ple_w_gate"], i)
    return x


def _prep_weights(norm_gains, mla_w_down, mla_q_norm, mla_kv_norm, mla_w_uq, mla_w_ukv, mla_w_o, swa_w_qkv,
                  swa_sink, swa_w_o, rel_bias, mlp_w_up, mlp_w_down, ple_w_up, ple_w_gate, ple_norm):
    wd, wq = zip(*[_prep_mla_weights(mla_w_down[j], mla_w_uq[j]) for j in range(mla_w_down.shape[0])])
    return dict(
        norm_gains=norm_gains, mla_w_down=wd, mla_q_norm=mla_q_norm, mla_kv_norm=mla_kv_norm, mla_w_uq=wq,
        mla_w_ukv=mla_w_ukv.astype(BF16), mla_w_o=mla_w_o.astype(BF16), swa_w_qkv=swa_w_qkv.astype(BF16),
        swa_sink=swa_sink, swa_w_o=swa_w_o.astype(BF16), bias_table=_bias_table(rel_bias),
        mlp_w_up=mlp_w_up.astype(BF16), mlp_w_down=mlp_w_down.astype(BF16), ple_w_up=ple_w_up.astype(BF16),
        ple_w_gate=ple_w_gate.astype(BF16), ple_norm=ple_norm)


def kernel(x_prompt, x_sample, p_prompt, p_sample, norm_gains, mla_w_down, mla_q_norm, mla_kv_norm, mla_w_uq,
           mla_w_ukv, mla_w_o, swa_w_qkv, swa_sink, swa_w_o, rel_bias, mlp_w_up, mlp_w_down, ple_w_up, ple_w_gate,
           ple_norm):
    w = _prep_weights(norm_gains, mla_w_down, mla_q_norm, mla_kv_norm, mla_w_uq, mla_w_ukv, mla_w_o, swa_w_qkv,
                      swa_sink, swa_w_o, rel_bias, mlp_w_up, mlp_w_down, ple_w_up, ple_w_gate, ple_norm)
    outs = []
    for x, p in ((x_prompt, p_prompt), (x_sample, p_sample)):
        b, s, d = x.shape
        y = _trunk(x.reshape(b * s, d), p.reshape(DEPTH, b * s, p.shape[-1]), b, s, w)
        outs.append(y.reshape(b, s, d))
    return tuple(outs)
```

```python
import functools

import numpy as np
import jax
import jax.numpy as jnp
from jax import lax
from jax.experimental import pallas as pl
from jax.experimental.pallas import tpu as pltpu

D_MODEL = 2048
DEPTH = 4
N_MIXERS = 2
MLA_HEADS = 16
Q_LORA = 512
KV_LORA = 512
NOPE_DIM = 128
ROPE_DIM = 64
V_DIM = 128
ROPE_THETA = 10000.0
SWA_Q_HEADS = 16
SWA_KV_HEADS = 4
SWA_GROUP = SWA_Q_HEADS // SWA_KV_HEADS
SWA_HEAD_DIM = 128
WINDOW = 128
BLOCK = 128
N_BUCKETS = 32
MAX_DISTANCE = 128
D_FF = 4 * D_MODEL
PLE_DIM = 256
EPS = 1e-6
NEG_INF = -1e30

LANES = 128
QK_PAD = 2 * LANES
ONES_COL = NOPE_DIM + ROPE_DIM
K2_COL = ONES_COL + 1
BOUND_MARGIN = 1.02
L_MIN = 2.0 ** -64
VMEM_LIMIT = 56 * 1024 * 1024

F32 = jnp.float32
BF16 = jnp.bfloat16


def _rms(x, g):
    return x * lax.rsqrt(jnp.mean(x * x, axis=-1, keepdims=True) + EPS) * g


def _dot(a, b):
    return jnp.dot(a, b, preferred_element_type=F32)


def _dot_nt(a, b):
    return lax.dot_general(a, b, (((1,), (1,)), ((), ())), preferred_element_type=F32)


def _const_spec(shape):
    nd = len(shape)
    return pl.BlockSpec(shape, lambda *_: (0,) * nd)


def _layer_spec(w, layer):
    return pl.BlockSpec((None,) + w.shape[1:], lambda *_: (layer, 0, 0))


def _params(sem):
    return pltpu.CompilerParams(dimension_semantics=sem, vmem_limit_bytes=VMEM_LIMIT)


def _norm_matmul_kernel(x_ref, g_ref, w_ref, o_ref, *, tn):
    h = _rms(x_ref[...], g_ref[...]).astype(BF16)
    for c in range(0, w_ref.shape[1], tn):
        o_ref[:, c:c + tn] = _dot(h, w_ref[:, c:c + tn]).astype(o_ref.dtype)


def _norm_matmul(x, g, w, layer, tm=512, tn=1024):
    t, d = x.shape
    n = w.shape[-1]
    return pl.pallas_call(
        functools.partial(_norm_matmul_kernel, tn=tn),
        grid=(t // tm,),
        in_specs=[pl.BlockSpec((tm, d), lambda i: (i, 0)), _const_spec((1, d)), _layer_spec(w, layer)],
        out_specs=pl.BlockSpec((tm, n), lambda i: (i, 0)),
        out_shape=jax.ShapeDtypeStruct((t, n), BF16),
        compiler_params=_params(("parallel",)),
        name="norm_matmul",
    )(x, g, w)


def _mla_proj_kernel(x_ref, g_ref, wd_ref, qn_ref, kvn_ref, wuq_ref, wukv_ref, cos_ref, sin_ref,
                     q_ref, k_ref, v_ref, *, scale):
    h = _rms(x_ref[...], g_ref[...]).astype(BF16)
    lat = _dot(h, wd_ref[...])
    cq = _rms(lat[:, :Q_LORA], qn_ref[...]).astype(BF16)
    ckv = _rms(lat[:, Q_LORA:Q_LORA + KV_LORA], kvn_ref[...]).astype(BF16)
    cos = cos_ref[...]
    sin = sin_ref[...]
    base = Q_LORA + KV_LORA
    kr = lat[:, base:base + LANES] * cos + lat[:, base + LANES:base + 2 * LANES] * sin
    kr2 = jnp.sum(kr * kr, axis=1, keepdims=True)
    lane = lax.broadcasted_iota(jnp.int32, kr.shape, 1)
    hn = MLA_HEADS * NOPE_DIM
    grp = 4
    cos_g = jnp.tile(cos, (1, grp))
    sin_g = jnp.tile(sin, (1, grp))
    for h0 in range(0, MLA_HEADS, grp):
        lo = h0 * LANES
        w = grp * LANES
        qn = _dot(cq, wuq_ref[:, lo:lo + w]) * scale
        qr = _dot(cq, wuq_ref[:, hn + lo:hn + lo + w])
        qs = _dot(cq, wuq_ref[:, 2 * hn + lo:2 * hn + lo + w])
        qrot = (qr * cos_g + qs * sin_g) * scale
        kv = _dot(ckv, wukv_ref[:, 2 * lo:2 * lo + 2 * w])
        for g in range(grp):
            hd = h0 + g
            c = g * LANES
            q_ref[hd, :LANES, :] = qn[:, c:c + LANES].T.astype(BF16)
            q_ref[hd, LANES:, :] = qrot[:, c:c + LANES].T.astype(BF16)
            kn = kv[:, 2 * c:2 * c + LANES]
            k_ref[hd, :, :LANES] = kn.astype(BF16)
            k2 = (jnp.sum(kn * kn, axis=1, keepdims=True) + kr2) * BOUND_MARGIN
            k_hi = jnp.where(lane == K2_COL - LANES, k2, kr)
            k_ref[hd, :, LANES:] = jnp.where(lane == ONES_COL - LANES, 1.0, k_hi).astype(BF16)
            v_ref[hd] = kv[:, 2 * c + LANES:2 * c + 2 * LANES].T.astype(BF16)


def _mla_proj(x, g, wd, qn, kvn, wuq, wukv, layer, cos, sin, seq, tm=256):
    t, d = x.shape
    nblk = seq // tm
    scale = float((NOPE_DIM + ROPE_DIM) ** -0.5 * np.log2(np.e))
    heads = MLA_HEADS
    return pl.pallas_call(
        functools.partial(_mla_proj_kernel, scale=scale),
        grid=(t // tm,),
        in_specs=[
            pl.BlockSpec((tm, d), lambda i: (i, 0)),
            _const_spec((1, d)),
            _const_spec(wd.shape),
            _const_spec((1, Q_LORA)),
            _const_spec((1, KV_LORA)),
            _const_spec(wuq.shape),
            _layer_spec(wukv, layer),
            pl.BlockSpec((tm, LANES), lambda i: (i % nblk, 0)),
            pl.BlockSpec((tm, LANES), lambda i: (i % nblk, 0)),
        ],
        out_specs=[
            pl.BlockSpec((heads, QK_PAD, tm), lambda i: (0, 0, i)),
            pl.BlockSpec((heads, tm, QK_PAD), lambda i: (0, i, 0)),
            pl.BlockSpec((heads, V_DIM, tm), lambda i: (0, 0, i)),
        ],
        out_shape=[
            jax.ShapeDtypeStruct((heads, QK_PAD, t), BF16),
            jax.ShapeDtypeStruct((heads, t, QK_PAD), BF16),
            jax.ShapeDtypeStruct((heads, V_DIM, t), BF16),
        ],
        compiler_params=_params(("parallel",)),
        name="mla_proj",
    )(x, g, wd, qn, kvn, wuq, wukv, cos, sin)


def _mla_attn_fixed_shift_kernel(qt_ref, k_ref, vt_ref, o_ref, l_ref, acc_sc, k2_sc, *, tkc, group):
    seq = k_ref.shape[0]
    qt = qt_ref[...]
    tq = qt.shape[1]

    @pl.when(pl.program_id(2) == 0)
    def _():
        colmax = jnp.max(k_ref[:, LANES:].astype(F32), axis=0, keepdims=True)
        lane = lax.broadcasted_iota(jnp.int32, colmax.shape, 1)
        k2max = jnp.max(jnp.where(lane == K2_COL - LANES, colmax, 0.0), axis=1, keepdims=True)
        k2_sc[...] = jnp.broadcast_to(k2max, k2_sc.shape)

    qf = qt.astype(F32)
    q2 = jnp.sum(qf * qf, axis=0, keepdims=True)
    shift = -(jnp.sqrt(q2 * jnp.tile(k2_sc[0:1, :], (1, tq // LANES))) * BOUND_MARGIN)
    row = lax.broadcasted_iota(jnp.int32, qt.shape, 0)
    q_aug = jnp.where(row == ONES_COL, shift.astype(BF16), qt)

    def chunk(c):
        off = pl.multiple_of(c * tkc, tkc)
        p = jnp.exp2(_dot(k_ref[pl.ds(off, tkc), :], q_aug))
        return _dot(vt_ref[:, pl.ds(off, tkc)], p.astype(BF16)), jnp.sum(p, axis=0, keepdims=True)

    def body(j, l_run):
        upd, l_new = chunk(group * j)
        for g in range(1, group):
            u, l = chunk(group * j + g)
            upd = upd + u
            l_new = l_new + l
        acc_sc[...] += upd
        return l_run + l_new

    acc_sc[...] = jnp.zeros(acc_sc.shape, F32)
    l = lax.fori_loop(0, seq // (group * tkc), body, jnp.zeros((1, tq), F32))
    l_ref[...] = l
    o_ref[...] = (acc_sc[...] * (1.0 / l)).T.astype(o_ref.dtype)


def _mla_attn_fixed_shift(qt, k, vt, batch, seq, tq=2048, tkc=1024, group=2):
    heads, _, t = qt.shape
    nq = seq // tq
    return pl.pallas_call(
        functools.partial(_mla_attn_fixed_shift_kernel, tkc=tkc, group=group),
        grid=(batch, heads, nq),
        in_specs=[
            pl.BlockSpec((None, QK_PAD, tq), lambda b, h, i: (h, 0, b * nq + i)),
            pl.BlockSpec((None, seq, QK_PAD), lambda b, h, i: (h, b, 0)),
            pl.BlockSpec((None, V_DIM, seq), lambda b, h, i: (h, 0, b)),
        ],
        out_specs=[
            pl.BlockSpec((tq, V_DIM), lambda b, h, i: (b * nq + i, h)),
            pl.BlockSpec((None, 1, tq), lambda b, h, i: (h, 0, b * nq + i)),
        ],
        out_shape=[jax.ShapeDtypeStruct((t, heads * V_DIM), BF16), jax.ShapeDtypeStruct((heads, 1, t), F32)],
        scratch_shapes=[pltpu.VMEM((V_DIM, tq), F32), pltpu.VMEM((8, LANES), F32)],
        compiler_params=_params(("parallel", "parallel", "arbitrary")),
        name="mla_attn_fixed_shift",
    )(qt, k, vt)


def _mla_attn_online_max_kernel(qt_ref, k_ref, vt_ref, o_ref, sa_sc, sb_sc, acc_sc, *, tkc):
    seq = k_ref.shape[0]
    npairs = seq // (2 * tkc)
    qt = qt_ref[...]
    tq = qt.shape[1]

    def scores(c, s_sc):
        off = pl.multiple_of(c * tkc, tkc)
        s = _dot(k_ref[pl.ds(off, tkc), :], qt)
        s_sc[...] = s
        return jnp.max(s, axis=0, keepdims=True)

    def accumulate(c, s_sc, m_run, l_run, m_chunk):
        off = pl.multiple_of(c * tkc, tkc)
        m_new = jnp.maximum(m_run, m_chunk)
        alpha = jnp.exp2(m_run - m_new)
        p = jnp.exp2(s_sc[...] - m_new)
        l_new = alpha * l_run + jnp.sum(p, axis=0, keepdims=True)
        pv = _dot(vt_ref[:, pl.ds(off, tkc)], p.astype(BF16))
        acc_sc[...] = alpha * acc_sc[...] + pv
        return m_new, l_new

    def pair(j, carry, last):
        m_run, l_run, m_a = carry
        m_b = scores(2 * j + 1, sb_sc)
        m_run, l_run = accumulate(2 * j, sa_sc, m_run, l_run, m_a)
        if not last:
            m_a = scores(2 * j + 2, sa_sc)
        m_run, l_run = accumulate(2 * j + 1, sb_sc, m_run, l_run, m_b)
        return m_run, l_run, m_a

    acc_sc[...] = jnp.zeros(acc_sc.shape, F32)
    init = (jnp.full((1, tq), -jnp.inf, F32), jnp.zeros((1, tq), F32), scores(0, sa_sc))
    carry = lax.fori_loop(0, npairs - 1, functools.partial(pair, last=False), init)
    _, l_run, _ = pair(npairs - 1, carry, last=True)
    o_ref[...] = (acc_sc[...] * (1.0 / l_run)).T.astype(o_ref.dtype)


def _mla_attn_online_max(qt, k, vt, batch, seq, tq=1024, tkc=1024):
    heads, _, t = qt.shape
    nq = seq // tq
    return pl.pallas_call(
        functools.partial(_mla_attn_online_max_kernel, tkc=tkc),
        grid=(batch, heads, nq),
        in_specs=[
            pl.BlockSpec((None, QK_PAD, tq), lambda b, h, i: (h, 0, b * nq + i)),
            pl.BlockSpec((None, seq, QK_PAD), lambda b, h, i: (h, b, 0)),
            pl.BlockSpec((None, V_DIM, seq), lambda b, h, i: (h, 0, b)),
        ],
        out_specs=pl.BlockSpec((tq, V_DIM), lambda b, h, i: (b * nq + i, h)),
        out_shape=jax.ShapeDtypeStruct((t, heads * V_DIM), BF16),
        scratch_shapes=[
            pltpu.VMEM((tkc, tq), F32),
            pltpu.VMEM((tkc, tq), F32),
            pltpu.VMEM((V_DIM, tq), F32),
        ],
        compiler_params=_params(("parallel", "parallel", "arbitrary")),
        name="mla_attn_online_max",
    )(qt, k, vt)


def _mla_attn(qt, k, vt, batch, seq):
    o, l = _mla_attn_fixed_shift(qt, k, vt, batch, seq)
    return lax.cond(jnp.min(l) >= L_MIN, lambda: o, lambda: _mla_attn_online_max(qt, k, vt, batch, seq))


def _t5_bucket(rel):
    nb = N_BUCKETS // 2
    max_exact = nb // 2
    ret = (rel > 0).astype(np.int32) * nb
    n = np.abs(rel)
    large = max_exact + (np.log(np.maximum(n, 1).astype(np.float32) / max_exact)
                         / np.log(MAX_DISTANCE / max_exact) * (nb - max_exact)).astype(np.int32)
    large = np.minimum(large, nb - 1)
    return (ret + np.where(n < max_exact, n, large)).astype(np.int32)


def _bias_table_kernel(rb_ref, bucket_ref, o_ref):
    hd = pl.program_id(0)
    bucket = bucket_ref[...]
    acc = jnp.zeros(bucket.shape, F32)
    for b in range(N_BUCKETS):
        acc = jnp.where(bucket == b, rb_ref[b, hd], acc)
    o_ref[...] = acc


def _bias_table(rel_bias):
    qi = np.arange(BLOCK)[:, None]
    si = np.arange(3 * BLOCK)[None, :]
    bucket = jnp.asarray(_t5_bucket(si - BLOCK - qi))
    return pl.pallas_call(
        _bias_table_kernel,
        grid=(SWA_Q_HEADS,),
        in_specs=[pl.BlockSpec(memory_space=pltpu.SMEM), _const_spec((BLOCK, 3 * BLOCK))],
        out_specs=pl.BlockSpec((None, BLOCK, 3 * BLOCK), lambda h: (h, 0, 0)),
        out_shape=jax.ShapeDtypeStruct((SWA_Q_HEADS, BLOCK, 3 * BLOCK), F32),
        name="t5_bias_table",
    )(rel_bias, bucket)


def _swa_attn_kernel(sink_ref, q_ref, kp_ref, kc_ref, kn_ref, vp_ref, vc_ref, vn_ref, bias_ref, o_ref,
                     *, nqb, nblocks):
    i = pl.program_id(1)
    dh = SWA_HEAD_DIM
    scale = float(dh ** -0.5)
    kband = jnp.concatenate([kp_ref[...], kc_ref[...], kn_ref[...]], axis=0)
    vband = jnp.concatenate([vp_ref[...], vc_ref[...], vn_ref[...]], axis=0)
    qi = lax.broadcasted_iota(jnp.int32, (BLOCK, 3 * BLOCK), 0)
    si = lax.broadcasted_iota(jnp.int32, (BLOCK, 3 * BLOCK), 1)
    rel = si - BLOCK - qi
    in_window = jnp.abs(rel) <= WINDOW
    for j in range(nqb):
        blk = i * nqb + j
        lo_ok = jnp.logical_or(si >= BLOCK, blk > 0)
        hi_ok = jnp.logical_or(si < 2 * BLOCK, blk < nblocks - 1)
        mask = in_window & lo_ok & hi_ok
        r0 = j * BLOCK
        for kh in range(SWA_KV_HEADS):
            kb = kband[r0:r0 + 3 * BLOCK, kh * dh:(kh + 1) * dh]
            vb = vband[r0:r0 + 3 * BLOCK, kh * dh:(kh + 1) * dh]
            heads = range(kh * SWA_GROUP, (kh + 1) * SWA_GROUP)
            qs = jnp.concatenate([q_ref[r0:r0 + BLOCK, hq * dh:(hq + 1) * dh] for hq in heads], axis=0)
            s_all = _dot_nt(qs, kb) * scale
            ps = []
            for g, hq in enumerate(heads):
                s = jnp.where(mask, s_all[g * BLOCK:(g + 1) * BLOCK] + bias_ref[hq], NEG_INF)
                sk = sink_ref[hq]
                m = jnp.maximum(jnp.max(s, axis=-1, keepdims=True), sk)
                e = jnp.exp(s - m)
                denom = jnp.sum(e, axis=-1, keepdims=True) + jnp.exp(sk - m)
                ps.append((e / denom).astype(BF16))
            o_all = _dot(jnp.concatenate(ps, axis=0), vb)
            for g, hq in enumerate(heads):
                o_ref[r0:r0 + BLOCK, hq * dh:(hq + 1) * dh] = o_all[g * BLOCK:(g + 1) * BLOCK].astype(o_ref.dtype)


def _swa_attn(qkv, sink, bias, batch, seq, tb=256):
    t = qkv.shape[0]
    nqb = tb // BLOCK
    nsteps = seq // tb
    nblocks = seq // BLOCK
    hq_w = SWA_Q_HEADS * SWA_HEAD_DIM
    kv_w = SWA_KV_HEADS * SWA_HEAD_DIM
    kcol = hq_w // kv_w
    vcol = kcol + 1

    def prev_row(b, i):
        return b * nblocks + jnp.maximum(i * nqb - 1, 0)

    def next_row(b, i):
        return b * nblocks + jnp.minimum((i + 1) * nqb, nblocks - 1)

    return pl.pallas_call(
        functools.partial(_swa_attn_kernel, nqb=nqb, nblocks=nblocks),
        grid=(batch, nsteps),
        in_specs=[
            pl.BlockSpec(memory_space=pltpu.SMEM),
            pl.BlockSpec((tb, hq_w), lambda b, i: (b * nsteps + i, 0)),
            pl.BlockSpec((BLOCK, kv_w), lambda b, i: (prev_row(b, i), kcol)),
            pl.BlockSpec((tb, kv_w), lambda b, i: (b * nsteps + i, kcol)),
            pl.BlockSpec((BLOCK, kv_w), lambda b, i: (next_row(b, i), kcol)),
            pl.BlockSpec((BLOCK, kv_w), lambda b, i: (prev_row(b, i), vcol)),
            pl.BlockSpec((tb, kv_w), lambda b, i: (b * nsteps + i, vcol)),
            pl.BlockSpec((BLOCK, kv_w), lambda b, i: (next_row(b, i), vcol)),
            _const_spec(bias.shape),
        ],
        out_specs=pl.BlockSpec((tb, hq_w), lambda b, i: (b * nsteps + i, 0)),
        out_shape=jax.ShapeDtypeStruct((t, hq_w), BF16),
        compiler_params=_params(("parallel", "parallel")),
        name="swa_attn",
    )(sink, qkv, qkv, qkv, qkv, qkv, qkv, qkv, bias)


def _out_proj_kernel(o_ref, w_ref, g_ref, x_ref, y_ref):
    y_ref[...] = x_ref[...] + _rms(_dot(o_ref[...], w_ref[...]), g_ref[...])


def _out_proj_residual(o, w, layer, g, x, tm=512):
    t, d = x.shape
    return pl.pallas_call(
        _out_proj_kernel,
        grid=(t // tm,),
        in_specs=[
            pl.BlockSpec((tm, o.shape[1]), lambda i: (i, 0)),
            _layer_spec(w, layer),
            _const_spec((1, d)),
            pl.BlockSpec((tm, d), lambda i: (i, 0)),
        ],
        out_specs=pl.BlockSpec((tm, d), lambda i: (i, 0)),
        out_shape=jax.ShapeDtypeStruct((t, d), F32),
        compiler_params=_params(("parallel",)),
        name="out_proj_residual",
    )(o, w, g, x)


def _mlp_kernel(x_ref, gin_ref, wup_ref, wdn_ref, gout_ref, y_ref, h_sc, acc_sc):
    j = pl.program_id(1)

    @pl.when(j == 0)
    def _():
        h_sc[...] = _rms(x_ref[...], gin_ref[...]).astype(BF16)
        acc_sc[...] = jnp.zeros(acc_sc.shape, F32)

    u = jnp.maximum(_dot(h_sc[...], wup_ref[...]), 0.0)
    acc_sc[...] += _dot((u * u).astype(BF16), wdn_ref[...])

    @pl.when(j == pl.num_programs(1) - 1)
    def _():
        y_ref[...] = x_ref[...] + _rms(acc_sc[...], gout_ref[...])


def _mlp(x, gin, wup, wdn, gout, layer, tm=512, tf=1024):
    t, d = x.shape
    f = wup.shape[-1]
    return pl.pallas_call(
        _mlp_kernel,
        grid=(t // tm, f // tf),
        in_specs=[
            pl.BlockSpec((tm, d), lambda i, j: (i, 0)),
            _const_spec((1, d)),
            pl.BlockSpec((None, d, tf), lambda i, j: (layer, 0, j)),
            pl.BlockSpec((None, tf, d), lambda i, j: (layer, j, 0)),
            _const_spec((1, d)),
        ],
        out_specs=pl.BlockSpec((tm, d), lambda i, j: (i, 0)),
        out_shape=jax.ShapeDtypeStruct((t, d), F32),
        scratch_shapes=[pltpu.VMEM((tm, d), BF16), pltpu.VMEM((tm, d), F32)],
        compiler_params=_params(("parallel", "arbitrary")),
        name="mlp",
    )(x, gin, wup, wdn, gout)


def _ple_kernel(x_ref, p_ref, wpu_ref, pn_ref, wg_ref, y_ref):
    x = x_ref[...]
    e = _rms(_dot(p_ref[...].astype(BF16), wpu_ref[...]), pn_ref[...])
    z = _dot(x.astype(BF16), wg_ref[...])
    y_ref[...] = x + e / (1.0 + jnp.exp(-z))


def _ple(x, p, wpu, pn, wg, layer, tm=512):
    t, d = x.shape
    return pl.pallas_call(
        _ple_kernel,
        grid=(t // tm,),
        in_specs=[
            pl.BlockSpec((tm, d), lambda i: (i, 0)),
            pl.BlockSpec((None, tm, p.shape[-1]), lambda i: (layer, i, 0)),
            _layer_spec(wpu, layer),
            _const_spec((1, d)),
            _layer_spec(wg, layer),
        ],
        out_specs=pl.BlockSpec((tm, d), lambda i: (i, 0)),
        out_shape=jax.ShapeDtypeStruct((t, d), F32),
        compiler_params=_params(("parallel",)),
        name="ple",
    )(x, p, wpu, pn, wg)


def _pad_cols(w, width):
    return jnp.pad(w, ((0, 0), (0, width - w.shape[1])))


def _swap_halves(w):
    half = w.shape[1] // 2
    return jnp.concatenate([w[:, half:], w[:, :half]], axis=1)


def _prep_mla_weights(w_down, w_uq):
    base = Q_LORA + KV_LORA
    w_kr = w_down[:, base:]
    wd = jnp.concatenate([w_down[:, :base], _pad_cols(w_kr, LANES), _pad_cols(_swap_halves(w_kr), LANES)], axis=1)
    wq = w_uq.reshape(Q_LORA, MLA_HEADS, NOPE_DIM + ROPE_DIM)
    nope = wq[:, :, :NOPE_DIM].reshape(Q_LORA, MLA_HEADS * NOPE_DIM)
    rope = wq[:, :, NOPE_DIM:]
    half = ROPE_DIM // 2
    swapped = jnp.concatenate([rope[:, :, half:], rope[:, :, :half]], axis=2)
    pad = ((0, 0), (0, 0), (0, LANES - ROPE_DIM))
    rope = jnp.pad(rope, pad).reshape(Q_LORA, MLA_HEADS * LANES)
    swapped = jnp.pad(swapped, pad).reshape(Q_LORA, MLA_HEADS * LANES)
    return wd.astype(BF16), jnp.concatenate([nope, rope, swapped], axis=1).astype(BF16)


def _rope_tables(seq):
    half = ROPE_DIM // 2
    inv = 1.0 / (ROPE_THETA ** (jnp.arange(half, dtype=F32) / half))
    ang = jnp.arange(seq).astype(F32)[:, None] * inv[None, :]
    cos = jnp.cos(ang)
    sin = jnp.sin(ang)
    return jnp.tile(cos, (1, LANES // half)), jnp.tile(jnp.concatenate([-sin, sin], axis=1), (1, LANES // ROPE_DIM))


def _trunk(x, p, batch, seq, w):
    cos, sin = _rope_tables(seq)
    for i in range(DEPTH):
        g = w["norm_gains"][i]
        j = i // N_MIXERS
        if i % N_MIXERS == 0:
            q, k, v = _mla_proj(x, g[0:1], w["mla_w_down"][j], w["mla_q_norm"][j:j + 1], w["mla_kv_norm"][j:j + 1],
                                w["mla_w_uq"][j], w["mla_w_ukv"], j, cos, sin, seq)
            o = _mla_attn(q, k, v, batch, seq)
            x = _out_proj_residual(o, w["mla_w_o"], j, g[1:2], x)
        else:
            qkv = _norm_matmul(x, g[0:1], w["swa_w_qkv"], j)
            o = _swa_attn(qkv, w["swa_sink"][j], w["bias_table"], batch, seq)
            x = _out_proj_residual(o, w["swa_w_o"], j, g[1:2], x)
        x = _mlp(x, g[2:3], w["mlp_w_up"], w["mlp_w_down"], g[3:4], i)
        x = _ple(x, p, w["ple_w_up"], w["ple_norm"][i:i + 1], w["ple_w_gate"], i)
    return x


def _prep_weights(norm_gains, mla_w_down, mla_q_norm, mla_kv_norm, mla_w_uq, mla_w_ukv, mla_w_o, swa_w_qkv,
                  swa_sink, swa_w_o, rel_bias, mlp_w_up, mlp_w_down, ple_w_up, ple_w_gate, ple_norm):
    wd, wq = zip(*[_prep_mla_weights(mla_w_down[j], mla_w_uq[j]) for j in range(mla_w_down.shape[0])])
    return dict(
        norm_gains=norm_gains, mla_w_down=wd, mla_q_norm=mla_q_norm, mla_kv_norm=mla_kv_norm, mla_w_uq=wq,
        mla_w_ukv=mla_w_ukv.astype(BF16), mla_w_o=mla_w_o.astype(BF16), swa_w_qkv=swa_w_qkv.astype(BF16),
        swa_sink=swa_sink, swa_w_o=swa_w_o.astype(BF16), bias_table=_bias_table(rel_bias),
        mlp_w_up=mlp_w_up.astype(BF16), mlp_w_down=mlp_w_down.astype(BF16), ple_w_up=ple_w_up.astype(BF16),
        ple_w_gate=ple_w_gate.astype(BF16), ple_norm=ple_norm)


def kernel(x_prompt, x_sample, p_prompt, p_sample, norm_gains, mla_w_down, mla_q_norm, mla_kv_norm, mla_w_uq,
           mla_w_ukv, mla_w_o, swa_w_qkv, swa_sink, swa_w_o, rel_bias, mlp_w_up, mlp_w_down, ple_w_up, ple_w_gate,
           ple_norm):
    w = _prep_weights(norm_gains, mla_w_down, mla_q_norm, mla_kv_norm, mla_w_uq, mla_w_ukv, mla_w_o, swa_w_qkv,
                      swa_sink, swa_w_o, rel_bias, mlp_w_up, mlp_w_down, ple_w_up, ple_w_gate, ple_norm)
    outs = []
    for x, p in ((x_prompt, p_prompt), (x_sample, p_sample)):
        b, s, d = x.shape
        y = _trunk(x.reshape(b * s, d), p.reshape(DEPTH, b * s, p.shape[-1]), b, s, w)
        outs.append(y.reshape(b, s, d))
    return tuple(outs)
```

```python
import functools

import numpy as np
import jax
import jax.numpy as jnp
from jax import lax
from jax.experimental import pallas as pl
from jax.experimental.pallas import tpu as pltpu

D_MODEL = 2048
DEPTH = 4
N_MIXERS = 2
MLA_HEADS = 16
Q_LORA = 512
KV_LORA = 512
NOPE_DIM = 128
ROPE_DIM = 64
V_DIM = 128
ROPE_THETA = 10000.0
SWA_Q_HEADS = 16
SWA_KV_HEADS = 4
SWA_GROUP = SWA_Q_HEADS // SWA_KV_HEADS
SWA_HEAD_DIM = 128
WINDOW = 128
BLOCK = 128
N_BUCKETS = 32
MAX_DISTANCE = 128
D_FF = 4 * D_MODEL
PLE_DIM = 256
EPS = 1e-6
NEG_INF = -1e30

LANES = 128
QK_PAD = 2 * LANES
ONES_COL = NOPE_DIM + ROPE_DIM
K2_COL = ONES_COL + 1
BOUND_MARGIN = 1.02
L_MIN = 2.0 ** -64
LOG2E = float(np.log2(np.e))
VMEM_LIMIT = 56 * 1024 * 1024

F32 = jnp.float32
BF16 = jnp.bfloat16


def _rms(x, g):
    return x * lax.rsqrt(jnp.mean(x * x, axis=-1, keepdims=True) + EPS) * g


def _dot(a, b):
    return jnp.dot(a, b, preferred_element_type=F32)


def _const_spec(shape):
    nd = len(shape)
    return pl.BlockSpec(shape, lambda *_: (0,) * nd)


def _layer_spec(w, layer):
    return pl.BlockSpec((None,) + w.shape[1:], lambda *_: (layer, 0, 0))


def _params(sem):
    return pltpu.CompilerParams(dimension_semantics=sem, vmem_limit_bytes=VMEM_LIMIT)


def _swa_proj_kernel(x_ref, g_ref, w_ref, qt_ref, k_ref, vt_ref, *, q_scale):
    h = _rms(x_ref[...], g_ref[...]).astype(BF16)
    dh = SWA_HEAD_DIM
    q_w = SWA_Q_HEADS * dh
    kv_w = SWA_KV_HEADS * dh
    for c in range(0, q_w, kv_w):
        y = _dot(h, w_ref[:, c:c + kv_w]) * q_scale
        for hq in range(kv_w // dh):
            qt_ref[c + hq * dh:c + (hq + 1) * dh, :] = y[:, hq * dh:(hq + 1) * dh].T.astype(BF16)
    k_ref[...] = _dot(h, w_ref[:, q_w:q_w + kv_w]).astype(BF16)
    v = _dot(h, w_ref[:, q_w + kv_w:])
    for kh in range(SWA_KV_HEADS):
        vt_ref[kh * dh:(kh + 1) * dh, :] = v[:, kh * dh:(kh + 1) * dh].T.astype(BF16)


def _swa_proj(x, g, w, layer, tm=512):
    t, d = x.shape
    q_w = SWA_Q_HEADS * SWA_HEAD_DIM
    kv_w = SWA_KV_HEADS * SWA_HEAD_DIM
    return pl.pallas_call(
        functools.partial(_swa_proj_kernel, q_scale=float(SWA_HEAD_DIM ** -0.5) * LOG2E),
        grid=(t // tm,),
        in_specs=[pl.BlockSpec((tm, d), lambda i: (i, 0)), _const_spec((1, d)), _layer_spec(w, layer)],
        out_specs=[
            pl.BlockSpec((q_w, tm), lambda i: (0, i)),
            pl.BlockSpec((tm, kv_w), lambda i: (i, 0)),
            pl.BlockSpec((kv_w, tm), lambda i: (0, i)),
        ],
        out_shape=[
            jax.ShapeDtypeStruct((q_w, t), BF16),
            jax.ShapeDtypeStruct((t, kv_w), BF16),
            jax.ShapeDtypeStruct((kv_w, t), BF16),
        ],
        compiler_params=_params(("parallel",)),
        name="swa_proj",
    )(x, g, w)


def _mla_proj_kernel(x_ref, g_ref, wd_ref, qn_ref, kvn_ref, wuq_ref, wukv_ref, cos_ref, sin_ref,
                     q_ref, k_ref, v_ref, *, scale):
    h = _rms(x_ref[...], g_ref[...]).astype(BF16)
    lat = _dot(h, wd_ref[...])
    cq = _rms(lat[:, :Q_LORA], qn_ref[...]).astype(BF16)
    ckv = _rms(lat[:, Q_LORA:Q_LORA + KV_LORA], kvn_ref[...]).astype(BF16)
    cos = cos_ref[...]
    sin = sin_ref[...]
    base = Q_LORA + KV_LORA
    kr = lat[:, base:base + LANES] * cos + lat[:, base + LANES:base + 2 * LANES] * sin
    kr2 = jnp.sum(kr * kr, axis=1, keepdims=True)
    lane = lax.broadcasted_iota(jnp.int32, kr.shape, 1)
    hn = MLA_HEADS * NOPE_DIM
    grp = 4
    cos_g = jnp.tile(cos, (1, grp))
    sin_g = jnp.tile(sin, (1, grp))
    for h0 in range(0, MLA_HEADS, grp):
        lo = h0 * LANES
        w = grp * LANES
        qn = _dot(cq, wuq_ref[:, lo:lo + w]) * scale
        qr = _dot(cq, wuq_ref[:, hn + lo:hn + lo + w])
        qs = _dot(cq, wuq_ref[:, 2 * hn + lo:2 * hn + lo + w])
        qrot = (qr * cos_g + qs * sin_g) * scale
        kv = _dot(ckv, wukv_ref[:, 2 * lo:2 * lo + 2 * w])
        for g in range(grp):
            hd = h0 + g
            c = g * LANES
            q_ref[hd, :LANES, :] = qn[:, c:c + LANES].T.astype(BF16)
            q_ref[hd, LANES:, :] = qrot[:, c:c + LANES].T.astype(BF16)
            kn = kv[:, 2 * c:2 * c + LANES]
            k_ref[hd, :, :LANES] = kn.astype(BF16)
            k2 = (jnp.sum(kn * kn, axis=1, keepdims=True) + kr2) * BOUND_MARGIN
            k_hi = jnp.where(lane == K2_COL - LANES, k2, kr)
            k_ref[hd, :, LANES:] = jnp.where(lane == ONES_COL - LANES, 1.0, k_hi).astype(BF16)
            v_ref[hd] = kv[:, 2 * c + LANES:2 * c + 2 * LANES].T.astype(BF16)


def _mla_proj(x, g, wd, qn, kvn, wuq, wukv, layer, cos, sin, seq, tm=256):
    t, d = x.shape
    nblk = seq // tm
    scale = float((NOPE_DIM + ROPE_DIM) ** -0.5 * np.log2(np.e))
    heads = MLA_HEADS
    return pl.pallas_call(
        functools.partial(_mla_proj_kernel, scale=scale),
        grid=(t // tm,),
        in_specs=[
            pl.BlockSpec((tm, d), lambda i: (i, 0)),
            _const_spec((1, d)),
            _const_spec(wd.shape),
            _const_spec((1, Q_LORA)),
            _const_spec((1, KV_LORA)),
            _const_spec(wuq.shape),
            _layer_spec(wukv, layer),
            pl.BlockSpec((tm, LANES), lambda i: (i % nblk, 0)),
            pl.BlockSpec((tm, LANES), lambda i: (i % nblk, 0)),
        ],
        out_specs=[
            pl.BlockSpec((heads, QK_PAD, tm), lambda i: (0, 0, i)),
            pl.BlockSpec((heads, tm, QK_PAD), lambda i: (0, i, 0)),
            pl.BlockSpec((heads, V_DIM, tm), lambda i: (0, 0, i)),
        ],
        out_shape=[
            jax.ShapeDtypeStruct((heads, QK_PAD, t), BF16),
            jax.ShapeDtypeStruct((heads, t, QK_PAD), BF16),
            jax.ShapeDtypeStruct((heads, V_DIM, t), BF16),
        ],
        compiler_params=_params(("parallel",)),
        name="mla_proj",
    )(x, g, wd, qn, kvn, wuq, wukv, cos, sin)


def _mla_attn_fixed_shift_kernel(qt_ref, k_ref, vt_ref, o_ref, l_ref, acc_sc, k2_sc, *, tkc, group):
    seq = k_ref.shape[0]
    qt = qt_ref[...]
    tq = qt.shape[1]

    @pl.when(pl.program_id(2) == 0)
    def _():
        colmax = jnp.max(k_ref[:, LANES:].astype(F32), axis=0, keepdims=True)
        lane = lax.broadcasted_iota(jnp.int32, colmax.shape, 1)
        k2max = jnp.max(jnp.where(lane == K2_COL - LANES, colmax, 0.0), axis=1, keepdims=True)
        k2_sc[...] = jnp.broadcast_to(k2max, k2_sc.shape)

    qf = qt.astype(F32)
    q2 = jnp.sum(qf * qf, axis=0, keepdims=True)
    shift = -(jnp.sqrt(q2 * jnp.tile(k2_sc[0:1, :], (1, tq // LANES))) * BOUND_MARGIN)
    row = lax.broadcasted_iota(jnp.int32, qt.shape, 0)
    q_aug = jnp.where(row == ONES_COL, shift.astype(BF16), qt)

    def chunk(c):
        off = pl.multiple_of(c * tkc, tkc)
        p = jnp.exp2(_dot(k_ref[pl.ds(off, tkc), :], q_aug))
        return _dot(vt_ref[:, pl.ds(off, tkc)], p.astype(BF16)), jnp.sum(p, axis=0, keepdims=True)

    def body(j, l_run):
        upd, l_new = chunk(group * j)
        for g in range(1, group):
            u, l = chunk(group * j + g)
            upd = upd + u
            l_new = l_new + l
        acc_sc[...] += upd
        return l_run + l_new

    acc_sc[...] = jnp.zeros(acc_sc.shape, F32)
    l = lax.fori_loop(0, seq // (group * tkc), body, jnp.zeros((1, tq), F32))
    l_ref[...] = l
    o_ref[...] = (acc_sc[...] * (1.0 / l)).T.astype(o_ref.dtype)


def _mla_attn_fixed_shift(qt, k, vt, batch, seq, tq=2048, tkc=1024, group=2):
    heads, _, t = qt.shape
    nq = seq // tq
    return pl.pallas_call(
        functools.partial(_mla_attn_fixed_shift_kernel, tkc=tkc, group=group),
        grid=(batch, heads, nq),
        in_specs=[
            pl.BlockSpec((None, QK_PAD, tq), lambda b, h, i: (h, 0, b * nq + i)),
            pl.BlockSpec((None, seq, QK_PAD), lambda b, h, i: (h, b, 0)),
            pl.BlockSpec((None, V_DIM, seq), lambda b, h, i: (h, 0, b)),
        ],
        out_specs=[
            pl.BlockSpec((tq, V_DIM), lambda b, h, i: (b * nq + i, h)),
            pl.BlockSpec((None, 1, tq), lambda b, h, i: (h, 0, b * nq + i)),
        ],
        out_shape=[jax.ShapeDtypeStruct((t, heads * V_DIM), BF16), jax.ShapeDtypeStruct((heads, 1, t), F32)],
        scratch_shapes=[pltpu.VMEM((V_DIM, tq), F32), pltpu.VMEM((8, LANES), F32)],
        compiler_params=_params(("parallel", "parallel", "arbitrary")),
        name="mla_attn_fixed_shift",
    )(qt, k, vt)


def _mla_attn_online_max_kernel(qt_ref, k_ref, vt_ref, o_ref, sa_sc, sb_sc, acc_sc, *, tkc):
    seq = k_ref.shape[0]
    npairs = seq // (2 * tkc)
    qt = qt_ref[...]
    tq = qt.shape[1]

    def scores(c, s_sc):
        off = pl.multiple_of(c * tkc, tkc)
        s = _dot(k_ref[pl.ds(off, tkc), :], qt)
        s_sc[...] = s
        return jnp.max(s, axis=0, keepdims=True)

    def accumulate(c, s_sc, m_run, l_run, m_chunk):
        off = pl.multiple_of(c * tkc, tkc)
        m_new = jnp.maximum(m_run, m_chunk)
        alpha = jnp.exp2(m_run - m_new)
        p = jnp.exp2(s_sc[...] - m_new)
        l_new = alpha * l_run + jnp.sum(p, axis=0, keepdims=True)
        pv = _dot(vt_ref[:, pl.ds(off, tkc)], p.astype(BF16))
        acc_sc[...] = alpha * acc_sc[...] + pv
        return m_new, l_new

    def pair(j, carry, last):
        m_run, l_run, m_a = carry
        m_b = scores(2 * j + 1, sb_sc)
        m_run, l_run = accumulate(2 * j, sa_sc, m_run, l_run, m_a)
        if not last:
            m_a = scores(2 * j + 2, sa_sc)
        m_run, l_run = accumulate(2 * j + 1, sb_sc, m_run, l_run, m_b)
        return m_run, l_run, m_a

    acc_sc[...] = jnp.zeros(acc_sc.shape, F32)
    init = (jnp.full((1, tq), -jnp.inf, F32), jnp.zeros((1, tq), F32), scores(0, sa_sc))
    carry = lax.fori_loop(0, npairs - 1, functools.partial(pair, last=False), init)
    _, l_run, _ = pair(npairs - 1, carry, last=True)
    o_ref[...] = (acc_sc[...] * (1.0 / l_run)).T.astype(o_ref.dtype)


def _mla_attn_online_max(qt, k, vt, batch, seq, tq=1024, tkc=1024):
    heads, _, t = qt.shape
    nq = seq // tq
    return pl.pallas_call(
        functools.partial(_mla_attn_online_max_kernel, tkc=tkc),
        grid=(batch, heads, nq),
        in_specs=[
            pl.BlockSpec((None, QK_PAD, tq), lambda b, h, i: (h, 0, b * nq + i)),
            pl.BlockSpec((None, seq, QK_PAD), lambda b, h, i: (h, b, 0)),
            pl.BlockSpec((None, V_DIM, seq), lambda b, h, i: (h, 0, b)),
        ],
        out_specs=pl.BlockSpec((tq, V_DIM), lambda b, h, i: (b * nq + i, h)),
        out_shape=jax.ShapeDtypeStruct((t, heads * V_DIM), BF16),
        scratch_shapes=[
            pltpu.VMEM((tkc, tq), F32),
            pltpu.VMEM((tkc, tq), F32),
            pltpu.VMEM((V_DIM, tq), F32),
        ],
        compiler_params=_params(("parallel", "parallel", "arbitrary")),
        name="mla_attn_online_max",
    )(qt, k, vt)


def _mla_attn(qt, k, vt, batch, seq):
    o, l = _mla_attn_fixed_shift(qt, k, vt, batch, seq)
    return lax.cond(jnp.min(l) >= L_MIN, lambda: o, lambda: _mla_attn_online_max(qt, k, vt, batch, seq))


def _t5_bucket(rel):
    nb = N_BUCKETS // 2
    max_exact = nb // 2
    ret = (rel > 0).astype(np.int32) * nb
    n = np.abs(rel)
    large = max_exact + (np.log(np.maximum(n, 1).astype(np.float32) / max_exact)
                         / np.log(MAX_DISTANCE / max_exact) * (nb - max_exact)).astype(np.int32)
    large = np.minimum(large, nb - 1)
    return (ret + np.where(n < max_exact, n, large)).astype(np.int32)


def _bias_table_kernel(rb_ref, bucket_ref, o_ref):
    hd = pl.program_id(0)
    bucket = bucket_ref[...]
    acc = jnp.zeros(bucket.shape, F32)
    for b in range(N_BUCKETS):
        acc = jnp.where(bucket == b, rb_ref[b, hd], acc)
    si = lax.broadcasted_iota(jnp.int32, bucket.shape, 0)
    qi = lax.broadcasted_iota(jnp.int32, bucket.shape, 1)
    o_ref[...] = jnp.where(jnp.abs(si - BLOCK - qi) <= WINDOW, acc * LOG2E, NEG_INF)


def _bias_table(rel_bias):
    si = np.arange(3 * BLOCK)[:, None]
    qi = np.arange(BLOCK)[None, :]
    bucket = jnp.asarray(_t5_bucket(si - BLOCK - qi))
    return pl.pallas_call(
        _bias_table_kernel,
        grid=(SWA_Q_HEADS,),
        in_specs=[pl.BlockSpec(memory_space=pltpu.SMEM), _const_spec((3 * BLOCK, BLOCK))],
        out_specs=pl.BlockSpec((None, 3 * BLOCK, BLOCK), lambda h: (h, 0, 0)),
        out_shape=jax.ShapeDtypeStruct((SWA_Q_HEADS, 3 * BLOCK, BLOCK), F32),
        name="t5_bias_table",
    )(rel_bias, bucket)


def _swa_attn_kernel(sink_ref, qt_ref, kp_ref, kc_ref, kn_ref, vp_ref, vc_ref, vn_ref, bias_ref, ot_ref,
                     sa_sc, sb_sc, *, nqb, nblocks):
    i = pl.program_id(1)
    dh = SWA_HEAD_DIM
    kband = jnp.concatenate([kp_ref[...], kc_ref[...], kn_ref[...]], axis=0)
    vband = jnp.concatenate([vp_ref[...], vc_ref[...], vn_ref[...]], axis=1)
    units = [(j, kh) for j in range(nqb) for kh in range(SWA_KV_HEADS)]
    bufs = (sa_sc, sb_sc)

    def scores(unit, s_sc):
        j, kh = unit
        r0 = j * BLOCK
        kb = kband[r0:r0 + 3 * BLOCK, kh * dh:(kh + 1) * dh]
        heads = range(kh * SWA_GROUP, (kh + 1) * SWA_GROUP)
        qt = jnp.concatenate([qt_ref[hq * dh:(hq + 1) * dh, r0:r0 + BLOCK] for hq in heads], axis=1)
        s_sc[...] = _dot(kb, qt)

    def run(at_sequence_edge):
        si = lax.broadcasted_iota(jnp.int32, (3 * BLOCK, 1), 0)
        scores(units[0], bufs[0])
        for u, (j, kh) in enumerate(units):
            if u + 1 < len(units):
                scores(units[u + 1], bufs[(u + 1) % 2])
            r0 = j * BLOCK
            vbt = vband[kh * dh:(kh + 1) * dh, r0:r0 + 3 * BLOCK]
            heads = range(kh * SWA_GROUP, (kh + 1) * SWA_GROUP)
            bias = jnp.concatenate([bias_ref[hq] for hq in heads], axis=1)
            s = bufs[u % 2][...] + bias
            if at_sequence_edge:
                blk = i * nqb + j
                lo_ok = jnp.logical_or(si >= BLOCK, blk > 0)
                hi_ok = jnp.logical_or(si < 2 * BLOCK, blk < nblocks - 1)
                s = s + jnp.where(lo_ok & hi_ok, 0.0, NEG_INF)
            sk = jnp.concatenate([jnp.full((1, BLOCK), sink_ref[hq] * LOG2E, F32) for hq in heads], axis=1)
            m = jnp.maximum(jnp.max(s, axis=0, keepdims=True), sk)
            e = jnp.exp2(s - m)
            denom = jnp.sum(e, axis=0, keepdims=True) + jnp.exp2(sk - m)
            ot = _dot(vbt, e.astype(BF16)) * (1.0 / denom)
            for g, hq in enumerate(heads):
                ot_ref[hq * dh:(hq + 1) * dh, r0:r0 + BLOCK] = ot[:, g * BLOCK:(g + 1) * BLOCK].astype(ot_ref.dtype)

    edge = jnp.logical_or(i == 0, i == pl.num_programs(1) - 1)
    pl.when(edge)(functools.partial(run, True))
    pl.when(jnp.logical_not(edge))(functools.partial(run, False))


def _swa_attn(qt, k, vt, sink, bias, batch, seq, tb=512):
    hq_w, t = qt.shape
    kv_w = k.shape[1]
    nqb = tb // BLOCK
    nsteps = seq // tb
    nblocks = seq // BLOCK

    def prev_blk(b, i):
        return b * nblocks + jnp.maximum(i * nqb - 1, 0)

    def next_blk(b, i):
        return b * nblocks + jnp.minimum((i + 1) * nqb, nblocks - 1)

    return pl.pallas_call(
        functools.partial(_swa_attn_kernel, nqb=nqb, nblocks=nblocks),
        grid=(batch, nsteps),
        in_specs=[
            pl.BlockSpec(memory_space=pltpu.SMEM),
            pl.BlockSpec((hq_w, tb), lambda b, i: (0, b * nsteps + i)),
            pl.BlockSpec((BLOCK, kv_w), lambda b, i: (prev_blk(b, i), 0)),
            pl.BlockSpec((tb, kv_w), lambda b, i: (b * nsteps + i, 0)),
            pl.BlockSpec((BLOCK, kv_w), lambda b, i: (next_blk(b, i), 0)),
            pl.BlockSpec((kv_w, BLOCK), lambda b, i: (0, prev_blk(b, i))),
            pl.BlockSpec((kv_w, tb), lambda b, i: (0, b * nsteps + i)),
            pl.BlockSpec((kv_w, BLOCK), lambda b, i: (0, next_blk(b, i))),
            _const_spec(bias.shape),
        ],
        out_specs=pl.BlockSpec((hq_w, tb), lambda b, i: (0, b * nsteps + i)),
        out_shape=jax.ShapeDtypeStruct((hq_w, t), BF16),
        scratch_shapes=[pltpu.VMEM((3 * BLOCK, SWA_GROUP * BLOCK), F32)] * 2,
        compiler_params=_params(("parallel", "parallel")),
        name="swa_attn",
    )(sink, qt, k, k, k, vt, vt, vt, bias)


def _out_proj_kernel(o_ref, w_ref, g_ref, x_ref, y_ref, *, feature_major):
    if feature_major:
        y = lax.dot_general(o_ref[...], w_ref[...], (((0,), (0,)), ((), ())), preferred_element_type=F32)
    else:
        y = _dot(o_ref[...], w_ref[...])
    y_ref[...] = x_ref[...] + _rms(y, g_ref[...])


def _out_proj_residual(o, w, layer, g, x, feature_major=False, tm=512):
    t, d = x.shape
    if feature_major:
        o_spec = pl.BlockSpec((o.shape[0], tm), lambda i: (0, i))
    else:
        o_spec = pl.BlockSpec((tm, o.shape[1]), lambda i: (i, 0))
    return pl.pallas_call(
        functools.partial(_out_proj_kernel, feature_major=feature_major),
        grid=(t // tm,),
        in_specs=[
            o_spec,
            _layer_spec(w, layer),
            _const_spec((1, d)),
            pl.BlockSpec((tm, d), lambda i: (i, 0)),
        ],
        out_specs=pl.BlockSpec((tm, d), lambda i: (i, 0)),
        out_shape=jax.ShapeDtypeStruct((t, d), F32),
        compiler_params=_params(("parallel",)),
        name="out_proj_residual",
    )(o, w, g, x)


def _mlp_kernel(x_ref, gin_ref, wup_ref, wdn_ref, gout_ref, y_ref, h_sc, acc_sc):
    j = pl.program_id(1)

    @pl.when(j == 0)
    def _():
        h_sc[...] = _rms(x_ref[...], gin_ref[...]).astype(BF16)
        acc_sc[...] = jnp.zeros(acc_sc.shape, F32)

    u = jnp.maximum(_dot(h_sc[...], wup_ref[...]), 0.0)
    acc_sc[...] += _dot((u * u).astype(BF16), wdn_ref[...])

    @pl.when(j == pl.num_programs(1) - 1)
    def _():
        y_ref[...] = x_ref[...] + _rms(acc_sc[...], gout_ref[...])


def _mlp(x, gin, wup, wdn, gout, layer, tm=512, tf=1024):
    t, d = x.shape
    f = wup.shape[-1]
    return pl.pallas_call(
        _mlp_kernel,
        grid=(t // tm, f // tf),
        in_specs=[
            pl.BlockSpec((tm, d), lambda i, j: (i, 0)),
            _const_spec((1, d)),
            pl.BlockSpec((None, d, tf), lambda i, j: (layer, 0, j)),
            pl.BlockSpec((None, tf, d), lambda i, j: (layer, j, 0)),
            _const_spec((1, d)),
        ],
        out_specs=pl.BlockSpec((tm, d), lambda i, j: (i, 0)),
        out_shape=jax.ShapeDtypeStruct((t, d), F32),
        scratch_shapes=[pltpu.VMEM((tm, d), BF16), pltpu.VMEM((tm, d), F32)],
        compiler_params=_params(("parallel", "arbitrary")),
        name="mlp",
    )(x, gin, wup, wdn, gout)


def _ple_kernel(x_ref, p_ref, wpu_ref, pn_ref, wg_ref, y_ref):
    x = x_ref[...]
    e = _rms(_dot(p_ref[...].astype(BF16), wpu_ref[...]), pn_ref[...])
    z = _dot(x.astype(BF16), wg_ref[...])
    y_ref[...] = x + e / (1.0 + jnp.exp(-z))


def _ple(x, p, wpu, pn, wg, layer, tm=512):
    t, d = x.shape
    return pl.pallas_call(
        _ple_kernel,
        grid=(t // tm,),
        in_specs=[
            pl.BlockSpec((tm, d), lambda i: (i, 0)),
            pl.BlockSpec((None, tm, p.shape[-1]), lambda i: (layer, i, 0)),
            _layer_spec(wpu, layer),
            _const_spec((1, d)),
            _layer_spec(wg, layer),
        ],
        out_specs=pl.BlockSpec((tm, d), lambda i: (i, 0)),
        out_shape=jax.ShapeDtypeStruct((t, d), F32),
        compiler_params=_params(("parallel",)),
        name="ple",
    )(x, p, wpu, pn, wg)


def _pad_cols(w, width):
    return jnp.pad(w, ((0, 0), (0, width - w.shape[1])))


def _swap_halves(w):
    half = w.shape[1] // 2
    return jnp.concatenate([w[:, half:], w[:, :half]], axis=1)


def _prep_mla_weights(w_down, w_uq):
    base = Q_LORA + KV_LORA
    w_kr = w_down[:, base:]
    wd = jnp.concatenate([w_down[:, :base], _pad_cols(w_kr, LANES), _pad_cols(_swap_halves(w_kr), LANES)], axis=1)
    wq = w_uq.reshape(Q_LORA, MLA_HEADS, NOPE_DIM + ROPE_DIM)
    nope = wq[:, :, :NOPE_DIM].reshape(Q_LORA, MLA_HEADS * NOPE_DIM)
    rope = wq[:, :, NOPE_DIM:]
    half = ROPE_DIM // 2
    swapped = jnp.concatenate([rope[:, :, half:], rope[:, :, :half]], axis=2)
    pad = ((0, 0), (0, 0), (0, LANES - ROPE_DIM))
    rope = jnp.pad(rope, pad).reshape(Q_LORA, MLA_HEADS * LANES)
    swapped = jnp.pad(swapped, pad).reshape(Q_LORA, MLA_HEADS * LANES)
    return wd.astype(BF16), jnp.concatenate([nope, rope, swapped], axis=1).astype(BF16)


def _rope_tables(seq):
    half = ROPE_DIM // 2
    inv = 1.0 / (ROPE_THETA ** (jnp.arange(half, dtype=F32) / half))
    ang = jnp.arange(seq).astype(F32)[:, None] * inv[None, :]
    cos = jnp.cos(ang)
    sin = jnp.sin(ang)
    return jnp.tile(cos, (1, LANES // half)), jnp.tile(jnp.concatenate([-sin, sin], axis=1), (1, LANES // ROPE_DIM))


def _trunk(x, p, batch, seq, w):
    cos, sin = _rope_tables(seq)
    for i in range(DEPTH):
        g = w["norm_gains"][i]
        j = i // N_MIXERS
        if i % N_MIXERS == 0:
            q, k, v = _mla_proj(x, g[0:1], w["mla_w_down"][j], w["mla_q_norm"][j:j + 1], w["mla_kv_norm"][j:j + 1],
                                w["mla_w_uq"][j], w["mla_w_ukv"], j, cos, sin, seq)
            o = _mla_attn(q, k, v, batch, seq)
            x = _out_proj_residual(o, w["mla_w_o"], j, g[1:2], x)
        else:
            qt, k, vt = _swa_proj(x, g[0:1], w["swa_w_qkv"], j)
            ot = _swa_attn(qt, k, vt, w["swa_sink"][j], w["bias_table"], batch, seq)
            x = _out_proj_residual(ot, w["swa_w_o"], j, g[1:2], x, feature_major=True)
        x = _mlp(x, g[2:3], w["mlp_w_up"], w["mlp_w_down"], g[3:4], i)
        x = _ple(x, p, w["ple_w_up"], w["ple_norm"][i:i + 1], w["ple_w_gate"], i)
    return x


def _prep_weights(norm_gains, mla_w_down, mla_q_norm, mla_kv_norm, mla_w_uq, mla_w_ukv, mla_w_o, swa_w_qkv,
                  swa_sink, swa_w_o, rel_bias, mlp_w_up, mlp_w_down, ple_w_up, ple_w_gate, ple_norm):
    wd, wq = zip(*[_prep_mla_weights(mla_w_down[j], mla_w_uq[j]) for j in range(mla_w_down.shape[0])])
    return dict(
        norm_gains=norm_gains, mla_w_down=wd, mla_q_norm=mla_q_norm, mla_kv_norm=mla_kv_norm, mla_w_uq=wq,
        mla_w_ukv=mla_w_ukv.astype(BF16), mla_w_o=mla_w_o.astype(BF16), swa_w_qkv=swa_w_qkv.astype(BF16),
        swa_sink=swa_sink, swa_w_o=swa_w_o.astype(BF16), bias_table=_bias_table(rel_bias),
        mlp_w_up=mlp_w_up.astype(BF16), mlp_w_down=mlp_w_down.astype(BF16), ple_w_up=ple_w_up.astype(BF16),
        ple_w_gate=ple_w_gate.astype(BF16), ple_norm=ple_norm)


def kernel(x_prompt, x_sample, p_prompt, p_sample, norm_gains, mla_w_down, mla_q_norm, mla_kv_norm, mla_w_uq,
           mla_w_ukv, mla_w_o, swa_w_qkv, swa_sink, swa_w_o, rel_bias, mlp_w_up, mlp_w_down, ple_w_up, ple_w_gate,
           ple_norm):
    w = _prep_weights(norm_gains, mla_w_down, mla_q_norm, mla_kv_norm, mla_w_uq, mla_w_ukv, mla_w_o, swa_w_qkv,
                      swa_sink, swa_w_o, rel_bias, mlp_w_up, mlp_w_down, ple_w_up, ple_w_gate, ple_norm)
    outs = []
    for x, p in ((x_prompt, p_prompt), (x_sample, p_sample)):
        b, s, d = x.shape
        y = _trunk(x.reshape(b * s, d), p.reshape(DEPTH, b * s, p.shape[-1]), b, s, w)
        outs.append(y.reshape(b, s, d))
    return tuple(outs)
```

```python
import functools

import numpy as np
import jax
import jax.numpy as jnp
from jax import lax
from jax.experimental import pallas as pl
from jax.experimental.pallas import tpu as pltpu

D_MODEL = 2048
DEPTH = 4
N_MIXERS = 2
MLA_HEADS = 16
Q_LORA = 512
KV_LORA = 512
NOPE_DIM = 128
ROPE_DIM = 64
V_DIM = 128
ROPE_THETA = 10000.0
SWA_Q_HEADS = 16
SWA_KV_HEADS = 4
SWA_GROUP = SWA_Q_HEADS // SWA_KV_HEADS
SWA_HEAD_DIM = 128
WINDOW = 128
BLOCK = 128
N_BUCKETS = 32
MAX_DISTANCE = 128
D_FF = 4 * D_MODEL
PLE_DIM = 256
EPS = 1e-6
NEG_INF = -1e30

LANES = 128
QK_PAD = 2 * LANES
ONES_COL = NOPE_DIM + ROPE_DIM
K2_COL = ONES_COL + 1
BOUND_MARGIN = 1.02
L_MIN = 2.0 ** -64
LOG2E = float(np.log2(np.e))
F32_ROWS = 8
BF16_ROWS = 16
VMEM_LIMIT = 56 * 1024 * 1024

F32 = jnp.float32
BF16 = jnp.bfloat16


def _rms(x, g):
    return x * lax.rsqrt(jnp.mean(x * x, axis=-1, keepdims=True) + EPS) * g


def _dot(a, b):
    return jnp.dot(a, b, preferred_element_type=F32)


def _const_spec(shape):
    nd = len(shape)
    return pl.BlockSpec(shape, lambda *_: (0,) * nd)


def _layer_spec(w, layer):
    return pl.BlockSpec((None,) + w.shape[1:], lambda *_: (layer, 0, 0))


def _params(sem):
    return pltpu.CompilerParams(dimension_semantics=sem, vmem_limit_bytes=VMEM_LIMIT)


def _swa_proj_kernel(x_ref, g_ref, w_ref, qt_ref, k_ref, vt_ref, *, q_scale):
    h = _rms(x_ref[...], g_ref[...]).astype(BF16)
    dh = SWA_HEAD_DIM
    q_w = SWA_Q_HEADS * dh
    kv_w = SWA_KV_HEADS * dh
    for c in range(0, q_w, kv_w):
        y = _dot(h, w_ref[:, c:c + kv_w]) * q_scale
        for hq in range(kv_w // dh):
            qt_ref[c + hq * dh:c + (hq + 1) * dh, :] = y[:, hq * dh:(hq + 1) * dh].T.astype(BF16)
    k_ref[...] = _dot(h, w_ref[:, q_w:q_w + kv_w]).astype(BF16)
    v = _dot(h, w_ref[:, q_w + kv_w:])
    for kh in range(SWA_KV_HEADS):
        vt_ref[kh * dh:(kh + 1) * dh, :] = v[:, kh * dh:(kh + 1) * dh].T.astype(BF16)


def _swa_proj(x, g, w, layer, tm=512):
    t, d = x.shape
    q_w = SWA_Q_HEADS * SWA_HEAD_DIM
    kv_w = SWA_KV_HEADS * SWA_HEAD_DIM
    return pl.pallas_call(
        functools.partial(_swa_proj_kernel, q_scale=float(SWA_HEAD_DIM ** -0.5) * LOG2E),
        grid=(t // tm,),
        in_specs=[pl.BlockSpec((tm, d), lambda i: (i, 0)), _const_spec((1, d)), _layer_spec(w, layer)],
        out_specs=[
            pl.BlockSpec((q_w, tm), lambda i: (0, i)),
            pl.BlockSpec((tm, kv_w), lambda i: (i, 0)),
            pl.BlockSpec((kv_w, tm), lambda i: (0, i)),
        ],
        out_shape=[
            jax.ShapeDtypeStruct((q_w, t), BF16),
            jax.ShapeDtypeStruct((t, kv_w), BF16),
            jax.ShapeDtypeStruct((kv_w, t), BF16),
        ],
        compiler_params=_params(("parallel",)),
        name="swa_proj",
    )(x, g, w)


def _mla_proj_kernel(x_ref, g_ref, wd_ref, qn_ref, kvn_ref, wuq_ref, wukv_ref, cos_ref, sin_ref,
                     q_ref, k_ref, v_ref, *, scale):
    h = _rms(x_ref[...], g_ref[...]).astype(BF16)
    lat = _dot(h, wd_ref[...])
    cq = _rms(lat[:, :Q_LORA], qn_ref[...]).astype(BF16)
    ckv = _rms(lat[:, Q_LORA:Q_LORA + KV_LORA], kvn_ref[...]).astype(BF16)
    cos = cos_ref[...]
    sin = sin_ref[...]
    base = Q_LORA + KV_LORA
    kr = lat[:, base:base + LANES] * cos + lat[:, base + LANES:base + 2 * LANES] * sin
    kr2 = jnp.sum(kr * kr, axis=1, keepdims=True)
    lane = lax.broadcasted_iota(jnp.int32, kr.shape, 1)
    hn = MLA_HEADS * NOPE_DIM
    grp = 4
    cos_g = jnp.tile(cos, (1, grp))
    sin_g = jnp.tile(sin, (1, grp))
    for h0 in range(0, MLA_HEADS, grp):
        lo = h0 * LANES
        w = grp * LANES
        qn = _dot(cq, wuq_ref[:, lo:lo + w]) * scale
        qr = _dot(cq, wuq_ref[:, hn + lo:hn + lo + w])
        qs = _dot(cq, wuq_ref[:, 2 * hn + lo:2 * hn + lo + w])
        qrot = (qr * cos_g + qs * sin_g) * scale
        kv = _dot(ckv, wukv_ref[:, 2 * lo:2 * lo + 2 * w])
        for g in range(grp):
            hd = h0 + g
            c = g * LANES
            q_ref[hd, :LANES, :] = qn[:, c:c + LANES].T.astype(BF16)
            q_ref[hd, LANES:, :] = qrot[:, c:c + LANES].T.astype(BF16)
            kn = kv[:, 2 * c:2 * c + LANES]
            k_ref[hd, :, :LANES] = kn.astype(BF16)
            k2 = (jnp.sum(kn * kn, axis=1, keepdims=True) + kr2) * BOUND_MARGIN
            k_hi = jnp.where(lane == K2_COL - LANES, k2, kr)
            k_ref[hd, :, LANES:] = jnp.where(lane == ONES_COL - LANES, 1.0, k_hi).astype(BF16)
            v_ref[hd] = kv[:, 2 * c + LANES:2 * c + 2 * LANES].T.astype(BF16)


def _mla_proj(x, g, wd, qn, kvn, wuq, wukv, layer, cos, sin, seq, tm=256):
    t, d = x.shape
    nblk = seq // tm
    scale = float((NOPE_DIM + ROPE_DIM) ** -0.5 * np.log2(np.e))
    heads = MLA_HEADS
    return pl.pallas_call(
        functools.partial(_mla_proj_kernel, scale=scale),
        grid=(t // tm,),
        in_specs=[
            pl.BlockSpec((tm, d), lambda i: (i, 0)),
            _const_spec((1, d)),
            _const_spec(wd.shape),
            _const_spec((1, Q_LORA)),
            _const_spec((1, KV_LORA)),
            _const_spec(wuq.shape),
            _layer_spec(wukv, layer),
            pl.BlockSpec((tm, LANES), lambda i: (i % nblk, 0)),
            pl.BlockSpec((tm, LANES), lambda i: (i % nblk, 0)),
        ],
        out_specs=[
            pl.BlockSpec((heads, QK_PAD, tm), lambda i: (0, 0, i)),
            pl.BlockSpec((heads, tm, QK_PAD), lambda i: (0, i, 0)),
            pl.BlockSpec((heads, V_DIM, tm), lambda i: (0, 0, i)),
        ],
        out_shape=[
            jax.ShapeDtypeStruct((heads, QK_PAD, t), BF16),
            jax.ShapeDtypeStruct((heads, t, QK_PAD), BF16),
            jax.ShapeDtypeStruct((heads, V_DIM, t), BF16),
        ],
        compiler_params=_params(("parallel",)),
        name="mla_proj",
    )(x, g, wd, qn, kvn, wuq, wukv, cos, sin)


def _mla_attn_fixed_shift_kernel(qt_ref, k_ref, vt_ref, o_ref, l_ref, acc_sc, k2_sc, *, tkc, group):
    seq = k_ref.shape[0]
    qt = qt_ref[...]
    tq = qt.shape[1]

    @pl.when(pl.program_id(2) == 0)
    def _():
        colmax = jnp.max(k_ref[:, LANES:].astype(F32), axis=0, keepdims=True)
        lane = lax.broadcasted_iota(jnp.int32, colmax.shape, 1)
        k2max = jnp.max(jnp.where(lane == K2_COL - LANES, colmax, 0.0), axis=1, keepdims=True)
        k2_sc[...] = jnp.broadcast_to(k2max, k2_sc.shape)

    qf = qt.astype(F32)
    q2 = jnp.sum(qf * qf, axis=0, keepdims=True)
    shift = -(jnp.sqrt(q2 * jnp.tile(k2_sc[0:1, :], (1, tq // LANES))) * BOUND_MARGIN)
    row = lax.broadcasted_iota(jnp.int32, qt.shape, 0)
    q_aug = jnp.where(row == ONES_COL, shift.astype(BF16), qt)

    def chunk(c):
        off = pl.multiple_of(c * tkc, tkc)
        p = jnp.exp2(_dot(k_ref[pl.ds(off, tkc), :], q_aug))
        return _dot(vt_ref[:, pl.ds(off, tkc)], p.astype(BF16)), jnp.sum(p, axis=0, keepdims=True)

    def body(j, l_run):
        upd, l_new = chunk(group * j)
        for g in range(1, group):
            u, l = chunk(group * j + g)
            upd = upd + u
            l_new = l_new + l
        acc_sc[...] += upd
        return l_run + l_new

    acc_sc[...] = jnp.zeros(acc_sc.shape, F32)
    l = lax.fori_loop(0, seq // (group * tkc), body, jnp.zeros((1, tq), F32))
    l_ref[...] = l
    o_ref[...] = (acc_sc[...] * (1.0 / l)).T.astype(o_ref.dtype)


def _mla_attn_fixed_shift(qt, k, vt, batch, seq, tq=2048, tkc=1024, group=2):
    heads, _, t = qt.shape
    nq = seq // tq
    return pl.pallas_call(
        functools.partial(_mla_attn_fixed_shift_kernel, tkc=tkc, group=group),
        grid=(batch, heads, nq),
        in_specs=[
            pl.BlockSpec((None, QK_PAD, tq), lambda b, h, i: (h, 0, b * nq + i)),
            pl.BlockSpec((None, seq, QK_PAD), lambda b, h, i: (h, b, 0)),
            pl.BlockSpec((None, V_DIM, seq), lambda b, h, i: (h, 0, b)),
        ],
        out_specs=[
            pl.BlockSpec((tq, V_DIM), lambda b, h, i: (b * nq + i, h)),
            pl.BlockSpec((None, 1, tq), lambda b, h, i: (h, 0, b * nq + i)),
        ],
        out_shape=[jax.ShapeDtypeStruct((t, heads * V_DIM), BF16), jax.ShapeDtypeStruct((heads, 1, t), F32)],
        scratch_shapes=[pltpu.VMEM((V_DIM, tq), F32), pltpu.VMEM((8, LANES), F32)],
        compiler_params=_params(("parallel", "parallel", "arbitrary")),
        name="mla_attn_fixed_shift",
    )(qt, k, vt)


def _mla_attn_online_max_kernel(qt_ref, k_ref, vt_ref, o_ref, sa_sc, sb_sc, acc_sc, *, tkc):
    seq = k_ref.shape[0]
    npairs = seq // (2 * tkc)
    qt = qt_ref[...]
    tq = qt.shape[1]

    def scores(c, s_sc):
        off = pl.multiple_of(c * tkc, tkc)
        s = _dot(k_ref[pl.ds(off, tkc), :], qt)
        s_sc[...] = s
        return jnp.max(s, axis=0, keepdims=True)

    def accumulate(c, s_sc, m_run, l_run, m_chunk):
        off = pl.multiple_of(c * tkc, tkc)
        m_new = jnp.maximum(m_run, m_chunk)
        alpha = jnp.exp2(m_run - m_new)
        p = jnp.exp2(s_sc[...] - m_new)
        l_new = alpha * l_run + jnp.sum(p, axis=0, keepdims=True)
        pv = _dot(vt_ref[:, pl.ds(off, tkc)], p.astype(BF16))
        acc_sc[...] = alpha * acc_sc[...] + pv
        return m_new, l_new

    def pair(j, carry, last):
        m_run, l_run, m_a = carry
        m_b = scores(2 * j + 1, sb_sc)
        m_run, l_run = accumulate(2 * j, sa_sc, m_run, l_run, m_a)
        if not last:
            m_a = scores(2 * j + 2, sa_sc)
        m_run, l_run = accumulate(2 * j + 1, sb_sc, m_run, l_run, m_b)
        return m_run, l_run, m_a

    acc_sc[...] = jnp.zeros(acc_sc.shape, F32)
    init = (jnp.full((1, tq), -jnp.inf, F32), jnp.zeros((1, tq), F32), scores(0, sa_sc))
    carry = lax.fori_loop(0, npairs - 1, functools.partial(pair, last=False), init)
    _, l_run, _ = pair(npairs - 1, carry, last=True)
    o_ref[...] = (acc_sc[...] * (1.0 / l_run)).T.astype(o_ref.dtype)


def _mla_attn_online_max(qt, k, vt, batch, seq, tq=1024, tkc=1024):
    heads, _, t = qt.shape
    nq = seq // tq
    return pl.pallas_call(
        functools.partial(_mla_attn_online_max_kernel, tkc=tkc),
        grid=(batch, heads, nq),
        in_specs=[
            pl.BlockSpec((None, QK_PAD, tq), lambda b, h, i: (h, 0, b * nq + i)),
            pl.BlockSpec((None, seq, QK_PAD), lambda b, h, i: (h, b, 0)),
            pl.BlockSpec((None, V_DIM, seq), lambda b, h, i: (h, 0, b)),
        ],
        out_specs=pl.BlockSpec((tq, V_DIM), lambda b, h, i: (b * nq + i, h)),
        out_shape=jax.ShapeDtypeStruct((t, heads * V_DIM), BF16),
        scratch_shapes=[
            pltpu.VMEM((tkc, tq), F32),
            pltpu.VMEM((tkc, tq), F32),
            pltpu.VMEM((V_DIM, tq), F32),
        ],
        compiler_params=_params(("parallel", "parallel", "arbitrary")),
        name="mla_attn_online_max",
    )(qt, k, vt)


def _mla_attn(qt, k, vt, batch, seq):
    o, l = _mla_attn_fixed_shift(qt, k, vt, batch, seq)
    return lax.cond(jnp.min(l) >= L_MIN, lambda: o, lambda: _mla_attn_online_max(qt, k, vt, batch, seq))


def _t5_bucket(rel):
    nb = N_BUCKETS // 2
    max_exact = nb // 2
    ret = (rel > 0).astype(np.int32) * nb
    n = np.abs(rel)
    large = max_exact + (np.log(np.maximum(n, 1).astype(np.float32) / max_exact)
                         / np.log(MAX_DISTANCE / max_exact) * (nb - max_exact)).astype(np.int32)
    large = np.minimum(large, nb - 1)
    return (ret + np.where(n < max_exact, n, large)).astype(np.int32)


def _bias_table_kernel(rb_ref, bucket_ref, o_ref):
    hd = pl.program_id(0)
    bucket = bucket_ref[...]
    acc = jnp.zeros(bucket.shape, F32)
    for b in range(N_BUCKETS):
        acc = jnp.where(bucket == b, rb_ref[b, hd], acc)
    si = lax.broadcasted_iota(jnp.int32, bucket.shape, 0)
    qi = lax.broadcasted_iota(jnp.int32, bucket.shape, 1)
    o_ref[...] = jnp.where(jnp.abs(si - BLOCK - qi) <= WINDOW, acc * LOG2E, NEG_INF)


def _bias_table(rel_bias):
    si = np.arange(3 * BLOCK)[:, None]
    qi = np.arange(BLOCK)[None, :]
    bucket = jnp.asarray(_t5_bucket(si - BLOCK - qi))
    return pl.pallas_call(
        _bias_table_kernel,
        grid=(SWA_Q_HEADS,),
        in_specs=[pl.BlockSpec(memory_space=pltpu.SMEM), _const_spec((3 * BLOCK, BLOCK))],
        out_specs=pl.BlockSpec((None, 3 * BLOCK, BLOCK), lambda h: (h, 0, 0)),
        out_shape=jax.ShapeDtypeStruct((SWA_Q_HEADS, 3 * BLOCK, BLOCK), F32),
        name="t5_bias_table",
    )(rel_bias, bucket)


def _swa_attn_kernel(sink_ref, qt_ref, kp_ref, kc_ref, kn_ref, vp_ref, vc_ref, vn_ref, bias_ref, ot_ref,
                     sa_sc, sb_sc, *, nqb, nblocks):
    i = pl.program_id(1)
    dh = SWA_HEAD_DIM
    kband = jnp.concatenate([kp_ref[...], kc_ref[...], kn_ref[...]], axis=0)
    vband = jnp.concatenate([vp_ref[...], vc_ref[...], vn_ref[...]], axis=1)
    units = [(j, kh) for j in range(nqb) for kh in range(SWA_KV_HEADS)]
    bufs = (sa_sc, sb_sc)

    def scores(unit, s_sc):
        j, kh = unit
        r0 = j * BLOCK
        kb = kband[r0:r0 + 3 * BLOCK, kh * dh:(kh + 1) * dh]
        heads = range(kh * SWA_GROUP, (kh + 1) * SWA_GROUP)
        qt = jnp.concatenate([qt_ref[hq * dh:(hq + 1) * dh, r0:r0 + BLOCK] for hq in heads], axis=1)
        s_sc[...] = _dot(kb, qt)

    def run(at_sequence_edge):
        si = lax.broadcasted_iota(jnp.int32, (3 * BLOCK, 1), 0)
        scores(units[0], bufs[0])
        for u, (j, kh) in enumerate(units):
            if u + 1 < len(units):
                scores(units[u + 1], bufs[(u + 1) % 2])
            r0 = j * BLOCK
            vbt = vband[kh * dh:(kh + 1) * dh, r0:r0 + 3 * BLOCK]
            heads = range(kh * SWA_GROUP, (kh + 1) * SWA_GROUP)
            bias = jnp.concatenate([bias_ref[hq] for hq in heads], axis=1)
            s = bufs[u % 2][...] + bias
            if at_sequence_edge:
                blk = i * nqb + j
                lo_ok = jnp.logical_or(si >= BLOCK, blk > 0)
                hi_ok = jnp.logical_or(si < 2 * BLOCK, blk < nblocks - 1)
                s = s + jnp.where(lo_ok & hi_ok, 0.0, NEG_INF)
            sk = jnp.concatenate([jnp.full((1, BLOCK), sink_ref[hq] * LOG2E, F32) for hq in heads], axis=1)
            m = jnp.maximum(jnp.max(s, axis=0, keepdims=True), sk)
            e = jnp.exp2(s - m)
            denom = jnp.sum(e, axis=0, keepdims=True) + jnp.exp2(sk - m)
            ot = _dot(vbt, e.astype(BF16)) * (1.0 / denom)
            for g, hq in enumerate(heads):
                ot_ref[hq * dh:(hq + 1) * dh, r0:r0 + BLOCK] = ot[:, g * BLOCK:(g + 1) * BLOCK].astype(ot_ref.dtype)

    edge = jnp.logical_or(i == 0, i == pl.num_programs(1) - 1)
    pl.when(edge)(functools.partial(run, True))
    pl.when(jnp.logical_not(edge))(functools.partial(run, False))


def _swa_attn(qt, k, vt, sink, bias, batch, seq, tb=512):
    hq_w, t = qt.shape
    kv_w = k.shape[1]
    nqb = tb // BLOCK
    nsteps = seq // tb
    nblocks = seq // BLOCK

    def prev_blk(b, i):
        return b * nblocks + jnp.maximum(i * nqb - 1, 0)

    def next_blk(b, i):
        return b * nblocks + jnp.minimum((i + 1) * nqb, nblocks - 1)

    return pl.pallas_call(
        functools.partial(_swa_attn_kernel, nqb=nqb, nblocks=nblocks),
        grid=(batch, nsteps),
        in_specs=[
            pl.BlockSpec(memory_space=pltpu.SMEM),
            pl.BlockSpec((hq_w, tb), lambda b, i: (0, b * nsteps + i)),
            pl.BlockSpec((BLOCK, kv_w), lambda b, i: (prev_blk(b, i), 0)),
            pl.BlockSpec((tb, kv_w), lambda b, i: (b * nsteps + i, 0)),
            pl.BlockSpec((BLOCK, kv_w), lambda b, i: (next_blk(b, i), 0)),
            pl.BlockSpec((kv_w, BLOCK), lambda b, i: (0, prev_blk(b, i))),
            pl.BlockSpec((kv_w, tb), lambda b, i: (0, b * nsteps + i)),
            pl.BlockSpec((kv_w, BLOCK), lambda b, i: (0, next_blk(b, i))),
            _const_spec(bias.shape),
        ],
        out_specs=pl.BlockSpec((hq_w, tb), lambda b, i: (0, b * nsteps + i)),
        out_shape=jax.ShapeDtypeStruct((hq_w, t), BF16),
        scratch_shapes=[pltpu.VMEM((3 * BLOCK, SWA_GROUP * BLOCK), F32)] * 2,
        compiler_params=_params(("parallel", "parallel")),
        name="swa_attn",
    )(sink, qt, k, k, k, vt, vt, vt, bias)


def _out_proj_kernel(o_ref, w_ref, g_ref, x_ref, y_ref, *, feature_major):
    if feature_major:
        y = lax.dot_general(o_ref[...], w_ref[...], (((0,), (0,)), ((), ())), preferred_element_type=F32)
    else:
        y = _dot(o_ref[...], w_ref[...])
    y_ref[...] = x_ref[...] + _rms(y, g_ref[...])


def _out_proj_residual(o, w, layer, g, x, feature_major=False, tm=512):
    t, d = x.shape
    if feature_major:
        o_spec = pl.BlockSpec((o.shape[0], tm), lambda i: (0, i))
    else:
        o_spec = pl.BlockSpec((tm, o.shape[1]), lambda i: (i, 0))
    return pl.pallas_call(
        functools.partial(_out_proj_kernel, feature_major=feature_major),
        grid=(t // tm,),
        in_specs=[
            o_spec,
            _layer_spec(w, layer),
            _const_spec((1, d)),
            pl.BlockSpec((tm, d), lambda i: (i, 0)),
        ],
        out_specs=pl.BlockSpec((tm, d), lambda i: (i, 0)),
        out_shape=jax.ShapeDtypeStruct((t, d), F32),
        compiler_params=_params(("parallel",)),
        name="out_proj_residual",
    )(o, w, g, x)


def _mlp_kernel(x_ref, xn_ref, gin_ref, wup_ref, wdn_ref, gout_ref, y_ref, h_sc, acc_sc):
    i = pl.program_id(0)
    j = pl.program_id(1)
    nj = pl.num_programs(1)
    rows = x_ref.shape[0] // nj

    @pl.when(jnp.logical_and(i == 0, j == 0))
    def _():
        h_sc[0] = _rms(x_ref[...], gin_ref[...]).astype(BF16)

    @pl.when(j == 0)
    def _():
        acc_sc[...] = jnp.zeros(acc_sc.shape, F32)

    u = jnp.maximum(_dot(h_sc[i % 2], wup_ref[...]), 0.0)
    acc_sc[...] += _dot((u * u).astype(BF16), wdn_ref[...])
    for r in range(0, rows, BF16_ROWS):
        r0 = pl.multiple_of(j * rows + r, BF16_ROWS)
        h_sc[(i + 1) % 2, pl.ds(r0, BF16_ROWS), :] = _rms(xn_ref[pl.ds(r0, BF16_ROWS), :], gin_ref[...]).astype(BF16)

    @pl.when(j == nj - 1)
    def _():
        g = gout_ref[...]
        for r in range(0, x_ref.shape[0], F32_ROWS):
            y_ref[r:r + F32_ROWS, :] = x_ref[r:r + F32_ROWS, :] + _rms(acc_sc[r:r + F32_ROWS, :], g)


def _mlp(x, gin, wup, wdn, gout, layer, tm=512, tf=1024):
    t, d = x.shape
    f = wup.shape[-1]
    nt = t // tm
    assert tm % (f // tf * BF16_ROWS) == 0
    return pl.pallas_call(
        _mlp_kernel,
        grid=(nt, f // tf),
        in_specs=[
            pl.BlockSpec((tm, d), lambda i, j: (i, 0)),
            pl.BlockSpec((tm, d), lambda i, j: (jnp.minimum(i + 1, nt - 1), 0)),
            _const_spec((1, d)),
            pl.BlockSpec((None, d, tf), lambda i, j: (layer, 0, j)),
            pl.BlockSpec((None, tf, d), lambda i, j: (layer, j, 0)),
            _const_spec((1, d)),
        ],
        out_specs=pl.BlockSpec((tm, d), lambda i, j: (i, 0)),
        out_shape=jax.ShapeDtypeStruct((t, d), F32),
        scratch_shapes=[pltpu.VMEM((2, tm, d), BF16), pltpu.VMEM((tm, d), F32)],
        compiler_params=_params(("arbitrary", "arbitrary")),
        name="mlp",
    )(x, x, gin, wup, wdn, gout)


def _ple_kernel(x_ref, p_ref, wpu_ref, pn_ref, wg_ref, y_ref):
    x = x_ref[...]
    e = _rms(_dot(p_ref[...].astype(BF16), wpu_ref[...]), pn_ref[...])
    z = _dot(x.astype(BF16), wg_ref[...])
    y_ref[...] = x + e / (1.0 + jnp.exp(-z))


def _ple(x, p, wpu, pn, wg, layer, tm=512):
    t, d = x.shape
    return pl.pallas_call(
        _ple_kernel,
        grid=(t // tm,),
        in_specs=[
            pl.BlockSpec((tm, d), lambda i: (i, 0)),
            pl.BlockSpec((None, tm, p.shape[-1]), lambda i: (layer, i, 0)),
            _layer_spec(wpu, layer),
            _const_spec((1, d)),
            _layer_spec(wg, layer),
        ],
        out_specs=pl.BlockSpec((tm, d), lambda i: (i, 0)),
        out_shape=jax.ShapeDtypeStruct((t, d), F32),
        compiler_params=_params(("parallel",)),
        name="ple",
    )(x, p, wpu, pn, wg)


def _pad_cols(w, width):
    return jnp.pad(w, ((0, 0), (0, width - w.shape[1])))


def _swap_halves(w):
    half = w.shape[1] // 2
    return jnp.concatenate([w[:, half:], w[:, :half]], axis=1)


def _prep_mla_weights(w_down, w_uq):
    base = Q_LORA + KV_LORA
    w_kr = w_down[:, base:]
    wd = jnp.concatenate([w_down[:, :base], _pad_cols(w_kr, LANES), _pad_cols(_swap_halves(w_kr), LANES)], axis=1)
    wq = w_uq.reshape(Q_LORA, MLA_HEADS, NOPE_DIM + ROPE_DIM)
    nope = wq[:, :, :NOPE_DIM].reshape(Q_LORA, MLA_HEADS * NOPE_DIM)
    rope = wq[:, :, NOPE_DIM:]
    half = ROPE_DIM // 2
    swapped = jnp.concatenate([rope[:, :, half:], rope[:, :, :half]], axis=2)
    pad = ((0, 0), (0, 0), (0, LANES - ROPE_DIM))
    rope = jnp.pad(rope, pad).reshape(Q_LORA, MLA_HEADS * LANES)
    swapped = jnp.pad(swapped, pad).reshape(Q_LORA, MLA_HEADS * LANES)
    return wd.astype(BF16), jnp.concatenate([nope, rope, swapped], axis=1).astype(BF16)


def _rope_tables(seq):
    half = ROPE_DIM // 2
    inv = 1.0 / (ROPE_THETA ** (jnp.arange(half, dtype=F32) / half))
    ang = jnp.arange(seq).astype(F32)[:, None] * inv[None, :]
    cos = jnp.cos(ang)
    sin = jnp.sin(ang)
    return jnp.tile(cos, (1, LANES // half)), jnp.tile(jnp.concatenate([-sin, sin], axis=1), (1, LANES // ROPE_DIM))


def _trunk(x, p, batch, seq, w):
    cos, sin = _rope_tables(seq)
    for i in range(DEPTH):
        g = w["norm_gains"][i]
        j = i // N_MIXERS
        if i % N_MIXERS == 0:
            q, k, v = _mla_proj(x, g[0:1], w["mla_w_down"][j], w["mla_q_norm"][j:j + 1], w["mla_kv_norm"][j:j + 1],
                                w["mla_w_uq"][j], w["mla_w_ukv"], j, cos, sin, seq)
            o = _mla_attn(q, k, v, batch, seq)
            x = _out_proj_residual(o, w["mla_w_o"], j, g[1:2], x)
        else:
            qt, k, vt = _swa_proj(x, g[0:1], w["swa_w_qkv"], j)
            ot = _swa_attn(qt, k, vt, w["swa_sink"][j], w["bias_table"], batch, seq)
            x = _out_proj_residual(ot, w["swa_w_o"], j, g[1:2], x, feature_major=True)
        x = _mlp(x, g[2:3], w["mlp_w_up"], w["mlp_w_down"], g[3:4], i)
        x = _ple(x, p, w["ple_w_up"], w["ple_norm"][i:i + 1], w["ple_w_gate"], i)
    return x


def _prep_weights(norm_gains, mla_w_down, mla_q_norm, mla_kv_norm, mla_w_uq, mla_w_ukv, mla_w_o, swa_w_qkv,
                  swa_sink, swa_w_o, rel_bias, mlp_w_up, mlp_w_down, ple_w_up, ple_w_gate, ple_norm):
    wd, wq = zip(*[_prep_mla_weights(mla_w_down[j], mla_w_uq[j]) for j in range(mla_w_down.shape[0])])
    return dict(
        norm_gains=norm_gains, mla_w_down=wd, mla_q_norm=mla_q_norm, mla_kv_norm=mla_kv_norm, mla_w_uq=wq,
        mla_w_ukv=mla_w_ukv.astype(BF16), mla_w_o=mla_w_o.astype(BF16), swa_w_qkv=swa_w_qkv.astype(BF16),
        swa_sink=swa_sink, swa_w_o=swa_w_o.astype(BF16), bias_table=_bias_table(rel_bias),
        mlp_w_up=mlp_w_up.astype(BF16), mlp_w_down=mlp_w_down.astype(BF16), ple_w_up=ple_w_up.astype(BF16),
        ple_w_gate=ple_w_gate.astype(BF16), ple_norm=ple_norm)


def kernel(x_prompt, x_sample, p_prompt, p_sample, norm_gains, mla_w_down, mla_q_norm, mla_kv_norm, mla_w_uq,
           mla_w_ukv, mla_w_o, swa_w_qkv, swa_sink, swa_w_o, rel_bias, mlp_w_up, mlp_w_down, ple_w_up, ple_w_gate,
           ple_norm):
    w = _prep_weights(norm_gains, mla_w_down, mla_q_norm, mla_kv_norm, mla_w_uq, mla_w_ukv, mla_w_o, swa_w_qkv,
                      swa_sink, swa_w_o, rel_bias, mlp_w_up, mlp_w_down, ple_w_up, ple_w_gate, ple_norm)
    outs = []
    for x, p in ((x_prompt, p_prompt), (x_sample, p_sample)):
        b, s, d = x.shape
        y = _trunk(x.reshape(b * s, d), p.reshape(DEPTH, b * s, p.shape[-1]), b, s, w)
        outs.append(y.reshape(b, s, d))
    return tuple(outs)
```

```python
import functools

import numpy as np
import jax
import jax.numpy as jnp
from jax import lax
from jax.experimental import pallas as pl
from jax.experimental.pallas import tpu as pltpu

D_MODEL = 2048
DEPTH = 4
N_MIXERS = 2
MLA_HEADS = 16
Q_LORA = 512
KV_LORA = 512
NOPE_DIM = 128
ROPE_DIM = 64
V_DIM = 128
ROPE_THETA = 10000.0
SWA_Q_HEADS = 16
SWA_KV_HEADS = 4
SWA_GROUP = SWA_Q_HEADS // SWA_KV_HEADS
SWA_HEAD_DIM = 128
WINDOW = 128
BLOCK = 128
N_BUCKETS = 32
MAX_DISTANCE = 128
D_FF = 4 * D_MODEL
PLE_DIM = 256
EPS = 1e-6
NEG_INF = -1e30

LANES = 128
QK_PAD = 2 * LANES
ONES_COL = NOPE_DIM + ROPE_DIM
K2_COL = ONES_COL + 1
BOUND_MARGIN = 1.02
L_MIN = 2.0 ** -64
LOG2E = float(np.log2(np.e))
F32_ROWS = 8
BF16_ROWS = 16
VMEM_LIMIT = 56 * 1024 * 1024

F32 = jnp.float32
BF16 = jnp.bfloat16


def _rms(x, g):
    return x * lax.rsqrt(jnp.mean(x * x, axis=-1, keepdims=True) + EPS) * g


def _dot(a, b):
    return jnp.dot(a, b, preferred_element_type=F32)


def _const_spec(shape):
    nd = len(shape)
    return pl.BlockSpec(shape, lambda *_: (0,) * nd)


def _layer_spec(w, layer):
    return pl.BlockSpec((None,) + w.shape[1:], lambda *_: (layer, 0, 0))


def _params(sem):
    return pltpu.CompilerParams(dimension_semantics=sem, vmem_limit_bytes=VMEM_LIMIT)


def _swa_proj_kernel(x_ref, g_ref, w_ref, qt_ref, k_ref, vt_ref, *, q_scale):
    h = _rms(x_ref[...], g_ref[...]).astype(BF16)
    dh = SWA_HEAD_DIM
    q_w = SWA_Q_HEADS * dh
    kv_w = SWA_KV_HEADS * dh
    for c in range(0, q_w, kv_w):
        y = _dot(h, w_ref[:, c:c + kv_w]) * q_scale
        for hq in range(kv_w // dh):
            qt_ref[c + hq * dh:c + (hq + 1) * dh, :] = y[:, hq * dh:(hq + 1) * dh].T.astype(BF16)
    k_ref[...] = _dot(h, w_ref[:, q_w:q_w + kv_w]).astype(BF16)
    v = _dot(h, w_ref[:, q_w + kv_w:])
    for kh in range(SWA_KV_HEADS):
        vt_ref[kh * dh:(kh + 1) * dh, :] = v[:, kh * dh:(kh + 1) * dh].T.astype(BF16)


def _swa_proj(x, g, w, layer, tm=512):
    t, d = x.shape
    q_w = SWA_Q_HEADS * SWA_HEAD_DIM
    kv_w = SWA_KV_HEADS * SWA_HEAD_DIM
    return pl.pallas_call(
        functools.partial(_swa_proj_kernel, q_scale=float(SWA_HEAD_DIM ** -0.5) * LOG2E),
        grid=(t // tm,),
        in_specs=[pl.BlockSpec((tm, d), lambda i: (i, 0)), _const_spec((1, d)), _layer_spec(w, layer)],
        out_specs=[
            pl.BlockSpec((q_w, tm), lambda i: (0, i)),
            pl.BlockSpec((tm, kv_w), lambda i: (i, 0)),
            pl.BlockSpec((kv_w, tm), lambda i: (0, i)),
        ],
        out_shape=[
            jax.ShapeDtypeStruct((q_w, t), BF16),
            jax.ShapeDtypeStruct((t, kv_w), BF16),
            jax.ShapeDtypeStruct((kv_w, t), BF16),
        ],
        compiler_params=_params(("parallel",)),
        name="swa_proj",
    )(x, g, w)


def _mla_proj_kernel(x_ref, g_ref, wd_ref, qn_ref, kvn_ref, wuq_ref, wukv_ref, cos_ref, sin_ref,
                     q_ref, k_ref, v_ref, *, scale):
    h = _rms(x_ref[...], g_ref[...]).astype(BF16)
    lat = _dot(h, wd_ref[...])
    cq = _rms(lat[:, :Q_LORA], qn_ref[...]).astype(BF16)
    ckv = _rms(lat[:, Q_LORA:Q_LORA + KV_LORA], kvn_ref[...]).astype(BF16)
    cos = cos_ref[...]
    sin = sin_ref[...]
    base = Q_LORA + KV_LORA
    kr = lat[:, base:base + LANES] * cos + lat[:, base + LANES:base + 2 * LANES] * sin
    kr2 = jnp.sum(kr * kr, axis=1, keepdims=True)
    lane = lax.broadcasted_iota(jnp.int32, kr.shape, 1)
    hn = MLA_HEADS * NOPE_DIM
    grp = 4
    rw = grp * ROPE_DIM
    hr = MLA_HEADS * ROPE_DIM
    cos_g = jnp.tile(cos, (1, rw // LANES))
    sin_g = jnp.tile(sin, (1, rw // LANES))
    zero_rows = jnp.zeros((QK_PAD - NOPE_DIM - ROPE_DIM, kr.shape[0]), BF16)
    for h0 in range(0, MLA_HEADS, grp):
        lo = h0 * LANES
        w = grp * LANES
        ro = hn + h0 * ROPE_DIM
        qn = _dot(cq, wuq_ref[:, lo:lo + w]) * scale
        qr = _dot(cq, wuq_ref[:, ro:ro + rw])
        qs = _dot(cq, wuq_ref[:, hr + ro:hr + ro + rw])
        qrot = (qr * cos_g + qs * sin_g) * scale
        kv = _dot(ckv, wukv_ref[:, 2 * lo:2 * lo + 2 * w])
        for g in range(grp):
            hd = h0 + g
            c = g * LANES
            q_ref[hd, :NOPE_DIM, :] = qn[:, c:c + LANES].T.astype(BF16)
            pair_t = qrot[:, (g // 2) * LANES:(g // 2 + 1) * LANES].T
            q_ref[hd, NOPE_DIM:ONES_COL, :] = pair_t[(g % 2) * ROPE_DIM:(g % 2 + 1) * ROPE_DIM].astype(BF16)
            q_ref[hd, ONES_COL:, :] = zero_rows
            kn = kv[:, 2 * c:2 * c + LANES]
            k_ref[hd, :, :LANES] = kn.astype(BF16)
            k2 = (jnp.sum(kn * kn, axis=1, keepdims=True) + kr2) * BOUND_MARGIN
            k_hi = jnp.where(lane == K2_COL - LANES, k2, kr)
            k_ref[hd, :, LANES:] = jnp.where(lane == ONES_COL - LANES, 1.0, k_hi).astype(BF16)
            v_ref[hd] = kv[:, 2 * c + LANES:2 * c + 2 * LANES].T.astype(BF16)


def _mla_proj(x, g, wd, qn, kvn, wuq, wukv, layer, cos, sin, seq, tm=256):
    t, d = x.shape
    nblk = seq // tm
    scale = float((NOPE_DIM + ROPE_DIM) ** -0.5 * np.log2(np.e))
    heads = MLA_HEADS
    return pl.pallas_call(
        functools.partial(_mla_proj_kernel, scale=scale),
        grid=(t // tm,),
        in_specs=[
            pl.BlockSpec((tm, d), lambda i: (i, 0)),
            _const_spec((1, d)),
            _const_spec(wd.shape),
            _const_spec((1, Q_LORA)),
            _const_spec((1, KV_LORA)),
            _const_spec(wuq.shape),
            _layer_spec(wukv, layer),
            pl.BlockSpec((tm, LANES), lambda i: (i % nblk, 0)),
            pl.BlockSpec((tm, LANES), lambda i: (i % nblk, 0)),
        ],
        out_specs=[
            pl.BlockSpec((heads, QK_PAD, tm), lambda i: (0, 0, i)),
            pl.BlockSpec((heads, tm, QK_PAD), lambda i: (0, i, 0)),
            pl.BlockSpec((heads, V_DIM, tm), lambda i: (0, 0, i)),
        ],
        out_shape=[
            jax.ShapeDtypeStruct((heads, QK_PAD, t), BF16),
            jax.ShapeDtypeStruct((heads, t, QK_PAD), BF16),
            jax.ShapeDtypeStruct((heads, V_DIM, t), BF16),
        ],
        compiler_params=_params(("parallel",)),
        name="mla_proj",
    )(x, g, wd, qn, kvn, wuq, wukv, cos, sin)


def _mla_attn_fixed_shift_kernel(qt_ref, k_ref, vt_ref, o_ref, l_ref, acc_sc, k2_sc, *, tkc, group):
    seq = k_ref.shape[0]
    qt = qt_ref[...]
    tq = qt.shape[1]

    @pl.when(pl.program_id(2) == 0)
    def _():
        colmax = jnp.max(k_ref[:, LANES:].astype(F32), axis=0, keepdims=True)
        lane = lax.broadcasted_iota(jnp.int32, colmax.shape, 1)
        k2max = jnp.max(jnp.where(lane == K2_COL - LANES, colmax, 0.0), axis=1, keepdims=True)
        k2_sc[...] = jnp.broadcast_to(k2max, k2_sc.shape)

    qf = qt.astype(F32)
    q2 = jnp.sum(qf * qf, axis=0, keepdims=True)
    shift = -(jnp.sqrt(q2 * jnp.tile(k2_sc[0:1, :], (1, tq // LANES))) * BOUND_MARGIN)
    row = lax.broadcasted_iota(jnp.int32, qt.shape, 0)
    q_aug = jnp.where(row == ONES_COL, shift.astype(BF16), qt)

    def chunk(c):
        off = pl.multiple_of(c * tkc, tkc)
        p = jnp.exp2(_dot(k_ref[pl.ds(off, tkc), :], q_aug))
        return _dot(vt_ref[:, pl.ds(off, tkc)], p.astype(BF16)), jnp.sum(p, axis=0, keepdims=True)

    def body(j, l_run):
        upd, l_new = chunk(group * j)
        for g in range(1, group):
            u, l = chunk(group * j + g)
            upd = upd + u
            l_new = l_new + l
        acc_sc[...] += upd
        return l_run + l_new

    acc_sc[...] = jnp.zeros(acc_sc.shape, F32)
    l = lax.fori_loop(0, seq // (group * tkc), body, jnp.zeros((1, tq), F32))
    l_ref[...] = l
    o_ref[...] = (acc_sc[...] * (1.0 / l)).T.astype(o_ref.dtype)


def _mla_attn_fixed_shift(qt, k, vt, batch, seq, tq=2048, tkc=1024, group=2):
    heads, _, t = qt.shape
    nq = seq // tq
    return pl.pallas_call(
        functools.partial(_mla_attn_fixed_shift_kernel, tkc=tkc, group=group),
        grid=(batch, heads, nq),
        in_specs=[
            pl.BlockSpec((None, QK_PAD, tq), lambda b, h, i: (h, 0, b * nq + i)),
            pl.BlockSpec((None, seq, QK_PAD), lambda b, h, i: (h, b, 0)),
            pl.BlockSpec((None, V_DIM, seq), lambda b, h, i: (h, 0, b)),
        ],
        out_specs=[
            pl.BlockSpec((tq, V_DIM), lambda b, h, i: (b * nq + i, h)),
            pl.BlockSpec((None, 1, tq), lambda b, h, i: (h, 0, b * nq + i)),
        ],
        out_shape=[jax.ShapeDtypeStruct((t, heads * V_DIM), BF16), jax.ShapeDtypeStruct((heads, 1, t), F32)],
        scratch_shapes=[pltpu.VMEM((V_DIM, tq), F32), pltpu.VMEM((8, LANES), F32)],
        compiler_params=_params(("parallel", "parallel", "arbitrary")),
        name="mla_attn_fixed_shift",
    )(qt, k, vt)


def _mla_attn_online_max_kernel(qt_ref, k_ref, vt_ref, o_ref, sa_sc, sb_sc, acc_sc, *, tkc):
    seq = k_ref.shape[0]
    npairs = seq // (2 * tkc)
    qt = qt_ref[...]
    tq = qt.shape[1]

    def scores(c, s_sc):
        off = pl.multiple_of(c * tkc, tkc)
        s = _dot(k_ref[pl.ds(off, tkc), :], qt)
        s_sc[...] = s
        return jnp.max(s, axis=0, keepdims=True)

    def accumulate(c, s_sc, m_run, l_run, m_chunk):
        off = pl.multiple_of(c * tkc, tkc)
        m_new = jnp.maximum(m_run, m_chunk)
        alpha = jnp.exp2(m_run - m_new)
        p = jnp.exp2(s_sc[...] - m_new)
        l_new = alpha * l_run + jnp.sum(p, axis=0, keepdims=True)
        pv = _dot(vt_ref[:, pl.ds(off, tkc)], p.astype(BF16))
        acc_sc[...] = alpha * acc_sc[...] + pv
        return m_new, l_new

    def pair(j, carry, last):
        m_run, l_run, m_a = carry
        m_b = scores(2 * j + 1, sb_sc)
        m_run, l_run = accumulate(2 * j, sa_sc, m_run, l_run, m_a)
        if not last:
            m_a = scores(2 * j + 2, sa_sc)
        m_run, l_run = accumulate(2 * j + 1, sb_sc, m_run, l_run, m_b)
        return m_run, l_run, m_a

    acc_sc[...] = jnp.zeros(acc_sc.shape, F32)
    init = (jnp.full((1, tq), -jnp.inf, F32), jnp.zeros((1, tq), F32), scores(0, sa_sc))
    carry = lax.fori_loop(0, npairs - 1, functools.partial(pair, last=False), init)
    _, l_run, _ = pair(npairs - 1, carry, last=True)
    o_ref[...] = (acc_sc[...] * (1.0 / l_run)).T.astype(o_ref.dtype)


def _mla_attn_online_max(qt, k, vt, batch, seq, tq=1024, tkc=1024):
    heads, _, t = qt.shape
    nq = seq // tq
    return pl.pallas_call(
        functools.partial(_mla_attn_online_max_kernel, tkc=tkc),
        grid=(batch, heads, nq),
        in_specs=[
            pl.BlockSpec((None, QK_PAD, tq), lambda b, h, i: (h, 0, b * nq + i)),
            pl.BlockSpec((None, seq, QK_PAD), lambda b, h, i: (h, b, 0)),
            pl.BlockSpec((None, V_DIM, seq), lambda b, h, i: (h, 0, b)),
        ],
        out_specs=pl.BlockSpec((tq, V_DIM), lambda b, h, i: (b * nq + i, h)),
        out_shape=jax.ShapeDtypeStruct((t, heads * V_DIM), BF16),
        scratch_shapes=[
            pltpu.VMEM((tkc, tq), F32),
            pltpu.VMEM((tkc, tq), F32),
            pltpu.VMEM((V_DIM, tq), F32),
        ],
        compiler_params=_params(("parallel", "parallel", "arbitrary")),
        name="mla_attn_online_max",
    )(qt, k, vt)


def _mla_attn(qt, k, vt, batch, seq):
    o, l = _mla_attn_fixed_shift(qt, k, vt, batch, seq)
    return lax.cond(jnp.min(l) >= L_MIN, lambda: o, lambda: _mla_attn_online_max(qt, k, vt, batch, seq))


def _t5_bucket(rel):
    nb = N_BUCKETS // 2
    max_exact = nb // 2
    ret = (rel > 0).astype(np.int32) * nb
    n = np.abs(rel)
    large = max_exact + (np.log(np.maximum(n, 1).astype(np.float32) / max_exact)
                         / np.log(MAX_DISTANCE / max_exact) * (nb - max_exact)).astype(np.int32)
    large = np.minimum(large, nb - 1)
    return (ret + np.where(n < max_exact, n, large)).astype(np.int32)


def _bias_table_kernel(rb_ref, bucket_ref, o_ref):
    hd = pl.program_id(0)
    bucket = bucket_ref[...]
    acc = jnp.zeros(bucket.shape, F32)
    for b in range(N_BUCKETS):
        acc = jnp.where(bucket == b, rb_ref[b, hd], acc)
    si = lax.broadcasted_iota(jnp.int32, bucket.shape, 0)
    qi = lax.broadcasted_iota(jnp.int32, bucket.shape, 1)
    o_ref[...] = jnp.where(jnp.abs(si - BLOCK - qi) <= WINDOW, acc * LOG2E, NEG_INF)


def _bias_table(rel_bias):
    si = np.arange(3 * BLOCK)[:, None]
    qi = np.arange(BLOCK)[None, :]
    bucket = jnp.asarray(_t5_bucket(si - BLOCK - qi))
    return pl.pallas_call(
        _bias_table_kernel,
        grid=(SWA_Q_HEADS,),
        in_specs=[pl.BlockSpec(memory_space=pltpu.SMEM), _const_spec((3 * BLOCK, BLOCK))],
        out_specs=pl.BlockSpec((None, 3 * BLOCK, BLOCK), lambda h: (h, 0, 0)),
        out_shape=jax.ShapeDtypeStruct((SWA_Q_HEADS, 3 * BLOCK, BLOCK), F32),
        name="t5_bias_table",
    )(rel_bias, bucket)


def _swa_attn_kernel(sink_ref, qt_ref, kp_ref, kc_ref, kn_ref, vp_ref, vc_ref, vn_ref, bias_ref, ot_ref,
                     sa_sc, sb_sc, *, nqb, nblocks):
    i = pl.program_id(1)
    dh = SWA_HEAD_DIM
    kband = jnp.concatenate([kp_ref[...], kc_ref[...], kn_ref[...]], axis=0)
    vband = jnp.concatenate([vp_ref[...], vc_ref[...], vn_ref[...]], axis=1)
    units = [(j, kh) for j in range(nqb) for kh in range(SWA_KV_HEADS)]
    bufs = (sa_sc, sb_sc)

    def scores(unit, s_sc):
        j, kh = unit
        r0 = j * BLOCK
        kb = kband[r0:r0 + 3 * BLOCK, kh * dh:(kh + 1) * dh]
        heads = range(kh * SWA_GROUP, (kh + 1) * SWA_GROUP)
        qt = jnp.concatenate([qt_ref[hq * dh:(hq + 1) * dh, r0:r0 + BLOCK] for hq in heads], axis=1)
        s_sc[...] = _dot(kb, qt)

    def run(at_sequence_edge):
        si = lax.broadcasted_iota(jnp.int32, (3 * BLOCK, 1), 0)
        scores(units[0], bufs[0])
        for u, (j, kh) in enumerate(units):
            if u + 1 < len(units):
                scores(units[u + 1], bufs[(u + 1) % 2])
            r0 = j * BLOCK
            vbt = vband[kh * dh:(kh + 1) * dh, r0:r0 + 3 * BLOCK]
            heads = range(kh * SWA_GROUP, (kh + 1) * SWA_GROUP)
            bias = jnp.concatenate([bias_ref[hq] for hq in heads], axis=1)
            s = bufs[u % 2][...] + bias
            if at_sequence_edge:
                blk = i * nqb + j
                lo_ok = jnp.logical_or(si >= BLOCK, blk > 0)
                hi_ok = jnp.logical_or(si < 2 * BLOCK, blk < nblocks - 1)
                s = s + jnp.where(lo_ok & hi_ok, 0.0, NEG_INF)
            sk = jnp.concatenate([jnp.full((1, BLOCK), sink_ref[hq] * LOG2E, F32) for hq in heads], axis=1)
            m = jnp.maximum(jnp.max(s, axis=0, keepdims=True), sk)
            e = jnp.exp2(s - m)
            denom = jnp.sum(e, axis=0, keepdims=True) + jnp.exp2(sk - m)
            ot = _dot(vbt, e.astype(BF16)) * (1.0 / denom)
            for g, hq in enumerate(heads):
                ot_ref[hq * dh:(hq + 1) * dh, r0:r0 + BLOCK] = ot[:, g * BLOCK:(g + 1) * BLOCK].astype(ot_ref.dtype)

    edge = jnp.logical_or(i == 0, i == pl.num_programs(1) - 1)
    pl.when(edge)(functools.partial(run, True))
    pl.when(jnp.logical_not(edge))(functools.partial(run, False))


def _swa_attn(qt, k, vt, sink, bias, batch, seq, tb=512):
    hq_w, t = qt.shape
    kv_w = k.shape[1]
    nqb = tb // BLOCK
    nsteps = seq // tb
    nblocks = seq // BLOCK

    def prev_blk(b, i):
        return b * nblocks + jnp.maximum(i * nqb - 1, 0)

    def next_blk(b, i):
        return b * nblocks + jnp.minimum((i + 1) * nqb, nblocks - 1)

    return pl.pallas_call(
        functools.partial(_swa_attn_kernel, nqb=nqb, nblocks=nblocks),
        grid=(batch, nsteps),
        in_specs=[
            pl.BlockSpec(memory_space=pltpu.SMEM),
            pl.BlockSpec((hq_w, tb), lambda b, i: (0, b * nsteps + i)),
            pl.BlockSpec((BLOCK, kv_w), lambda b, i: (prev_blk(b, i), 0)),
            pl.BlockSpec((tb, kv_w), lambda b, i: (b * nsteps + i, 0)),
            pl.BlockSpec((BLOCK, kv_w), lambda b, i: (next_blk(b, i), 0)),
            pl.BlockSpec((kv_w, BLOCK), lambda b, i: (0, prev_blk(b, i))),
            pl.BlockSpec((kv_w, tb), lambda b, i: (0, b * nsteps + i)),
            pl.BlockSpec((kv_w, BLOCK), lambda b, i: (0, next_blk(b, i))),
            _const_spec(bias.shape),
        ],
        out_specs=pl.BlockSpec((hq_w, tb), lambda b, i: (0, b * nsteps + i)),
        out_shape=jax.ShapeDtypeStruct((hq_w, t), BF16),
        scratch_shapes=[pltpu.VMEM((3 * BLOCK, SWA_GROUP * BLOCK), F32)] * 2,
        compiler_params=_params(("parallel", "parallel")),
        name="swa_attn",
    )(sink, qt, k, k, k, vt, vt, vt, bias)


def _out_proj_kernel(o_ref, w_ref, g_ref, x_ref, y_ref, *, feature_major):
    if feature_major:
        y = lax.dot_general(o_ref[...], w_ref[...], (((0,), (0,)), ((), ())), preferred_element_type=F32)
    else:
        y = _dot(o_ref[...], w_ref[...])
    y_ref[...] = x_ref[...] + _rms(y, g_ref[...])


def _out_proj_residual(o, w, layer, g, x, feature_major=False, tm=512):
    t, d = x.shape
    if feature_major:
        o_spec = pl.BlockSpec((o.shape[0], tm), lambda i: (0, i))
    else:
        o_spec = pl.BlockSpec((tm, o.shape[1]), lambda i: (i, 0))
    return pl.pallas_call(
        functools.partial(_out_proj_kernel, feature_major=feature_major),
        grid=(t // tm,),
        in_specs=[
            o_spec,
            _layer_spec(w, layer),
            _const_spec((1, d)),
            pl.BlockSpec((tm, d), lambda i: (i, 0)),
        ],
        out_specs=pl.BlockSpec((tm, d), lambda i: (i, 0)),
        out_shape=jax.ShapeDtypeStruct((t, d), F32),
        compiler_params=_params(("parallel",)),
        name="out_proj_residual",
    )(o, w, g, x)


def _mlp_kernel(x_ref, xn_ref, gin_ref, wup_ref, wdn_ref, gout_ref, y_ref, h_sc, acc_sc):
    i = pl.program_id(0)
    j = pl.program_id(1)
    nj = pl.num_programs(1)
    rows = x_ref.shape[0] // nj

    @pl.when(jnp.logical_and(i == 0, j == 0))
    def _():
        h_sc[0] = _rms(x_ref[...], gin_ref[...]).astype(BF16)

    @pl.when(j == 0)
    def _():
        acc_sc[...] = jnp.zeros(acc_sc.shape, F32)

    u = jnp.maximum(_dot(h_sc[i % 2], wup_ref[...]), 0.0)
    acc_sc[...] += _dot((u * u).astype(BF16), wdn_ref[...])
    for r in range(0, rows, BF16_ROWS):
        r0 = pl.multiple_of(j * rows + r, BF16_ROWS)
        h_sc[(i + 1) % 2, pl.ds(r0, BF16_ROWS), :] = _rms(xn_ref[pl.ds(r0, BF16_ROWS), :], gin_ref[...]).astype(BF16)

    @pl.when(j == nj - 1)
    def _():
        g = gout_ref[...]
        for r in range(0, x_ref.shape[0], F32_ROWS):
            y_ref[r:r + F32_ROWS, :] = x_ref[r:r + F32_ROWS, :] + _rms(acc_sc[r:r + F32_ROWS, :], g)


def _mlp(x, gin, wup, wdn, gout, layer, tm=512, tf=1024):
    t, d = x.shape
    f = wup.shape[-1]
    nt = t // tm
    assert tm % (f // tf * BF16_ROWS) == 0
    return pl.pallas_call(
        _mlp_kernel,
        grid=(nt, f // tf),
        in_specs=[
            pl.BlockSpec((tm, d), lambda i, j: (i, 0)),
            pl.BlockSpec((tm, d), lambda i, j: (jnp.minimum(i + 1, nt - 1), 0)),
            _const_spec((1, d)),
            pl.BlockSpec((None, d, tf), lambda i, j: (layer, 0, j)),
            pl.BlockSpec((None, tf, d), lambda i, j: (layer, j, 0)),
            _const_spec((1, d)),
        ],
        out_specs=pl.BlockSpec((tm, d), lambda i, j: (i, 0)),
        out_shape=jax.ShapeDtypeStruct((t, d), F32),
        scratch_shapes=[pltpu.VMEM((2, tm, d), BF16), pltpu.VMEM((tm, d), F32)],
        compiler_params=_params(("arbitrary", "arbitrary")),
        name="mlp",
    )(x, x, gin, wup, wdn, gout)


def _ple_kernel(x_ref, p_ref, wpu_ref, pn_ref, wg_ref, y_ref):
    x = x_ref[...]
    e = _rms(_dot(p_ref[...].astype(BF16), wpu_ref[...]), pn_ref[...])
    z = _dot(x.astype(BF16), wg_ref[...])
    y_ref[...] = x + e / (1.0 + jnp.exp(-z))


def _ple(x, p, wpu, pn, wg, layer, tm=512):
    t, d = x.shape
    return pl.pallas_call(
        _ple_kernel,
        grid=(t // tm,),
        in_specs=[
            pl.BlockSpec((tm, d), lambda i: (i, 0)),
            pl.BlockSpec((None, tm, p.shape[-1]), lambda i: (layer, i, 0)),
            _layer_spec(wpu, layer),
            _const_spec((1, d)),
            _layer_spec(wg, layer),
        ],
        out_specs=pl.BlockSpec((tm, d), lambda i: (i, 0)),
        out_shape=jax.ShapeDtypeStruct((t, d), F32),
        compiler_params=_params(("parallel",)),
        name="ple",
    )(x, p, wpu, pn, wg)


def _pad_cols(w, width):
    return jnp.pad(w, ((0, 0), (0, width - w.shape[1])))


def _swap_halves(w):
    half = w.shape[1] // 2
    return jnp.concatenate([w[:, half:], w[:, :half]], axis=1)


def _prep_mla_weights(w_down, w_uq):
    base = Q_LORA + KV_LORA
    w_kr = w_down[:, base:]
    wd = jnp.concatenate([w_down[:, :base], _pad_cols(w_kr, LANES), _pad_cols(_swap_halves(w_kr), LANES)], axis=1)
    wq = w_uq.reshape(Q_LORA, MLA_HEADS, NOPE_DIM + ROPE_DIM)
    nope = wq[:, :, :NOPE_DIM].reshape(Q_LORA, MLA_HEADS * NOPE_DIM)
    rope = wq[:, :, NOPE_DIM:]
    half = ROPE_DIM // 2
    swapped = jnp.concatenate([rope[:, :, half:], rope[:, :, :half]], axis=2)
    rope = rope.reshape(Q_LORA, MLA_HEADS * ROPE_DIM)
    swapped = swapped.reshape(Q_LORA, MLA_HEADS * ROPE_DIM)
    return wd.astype(BF16), jnp.concatenate([nope, rope, swapped], axis=1).astype(BF16)


def _rope_tables(seq):
    half = ROPE_DIM // 2
    inv = 1.0 / (ROPE_THETA ** (jnp.arange(half, dtype=F32) / half))
    ang = jnp.arange(seq).astype(F32)[:, None] * inv[None, :]
    cos = jnp.cos(ang)
    sin = jnp.sin(ang)
    return jnp.tile(cos, (1, LANES // half)), jnp.tile(jnp.concatenate([-sin, sin], axis=1), (1, LANES // ROPE_DIM))


def _trunk(x, p, batch, seq, w):
    cos, sin = _rope_tables(seq)
    for i in range(DEPTH):
        g = w["norm_gains"][i]
        j = i // N_MIXERS
        if i % N_MIXERS == 0:
            q, k, v = _mla_proj(x, g[0:1], w["mla_w_down"][j], w["mla_q_norm"][j:j + 1], w["mla_kv_norm"][j:j + 1],
                                w["mla_w_uq"][j], w["mla_w_ukv"], j, cos, sin, seq)
            o = _mla_attn(q, k, v, batch, seq)
            x = _out_proj_residual(o, w["mla_w_o"], j, g[1:2], x)
        else:
            qt, k, vt = _swa_proj(x, g[0:1], w["swa_w_qkv"], j)
            ot = _swa_attn(qt, k, vt, w["swa_sink"][j], w["bias_table"], batch, seq)
            x = _out_proj_residual(ot, w["swa_w_o"], j, g[1:2], x, feature_major=True)
        x = _mlp(x, g[2:3], w["mlp_w_up"], w["mlp_w_down"], g[3:4], i)
        x = _ple(x, p, w["ple_w_up"], w["ple_norm"][i:i + 1], w["ple_w_gate"], i)
    return x


def _prep_weights(norm_gains, mla_w_down, mla_q_norm, mla_kv_norm, mla_w_uq, mla_w_ukv, mla_w_o, swa_w_qkv,
                  swa_sink, swa_w_o, rel_bias, mlp_w_up, mlp_w_down, ple_w_up, ple_w_gate, ple_norm):
    wd, wq = zip(*[_prep_mla_weights(mla_w_down[j], mla_w_uq[j]) for j in range(mla_w_down.shape[0])])
    return dict(
        norm_gains=norm_gains, mla_w_down=wd, mla_q_norm=mla_q_norm, mla_kv_norm=mla_kv_norm, mla_w_uq=wq,
        mla_w_ukv=mla_w_ukv.astype(BF16), mla_w_o=mla_w_o.astype(BF16), swa_w_qkv=swa_w_qkv.astype(BF16),
        swa_sink=swa_sink, swa_w_o=swa_w_o.astype(BF16), bias_table=_bias_table(rel_bias),
        mlp_w_up=mlp_w_up.astype(BF16), mlp_w_down=mlp_w_down.astype(BF16), ple_w_up=ple_w_up.astype(BF16),
        ple_w_gate=ple_w_gate.astype(BF16), ple_norm=ple_norm)


def kernel(x_prompt, x_sample, p_prompt, p_sample, norm_gains, mla_w_down, mla_q_norm, mla_kv_norm, mla_w_uq,
           mla_w_ukv, mla_w_o, swa_w_qkv, swa_sink, swa_w_o, rel_bias, mlp_w_up, mlp_w_down, ple_w_up, ple_w_gate,
           ple_norm):
    w = _prep_weights(norm_gains, mla_w_down, mla_q_norm, mla_kv_norm, mla_w_uq, mla_w_ukv, mla_w_o, swa_w_qkv,
                      swa_sink, swa_w_o, rel_bias, mlp_w_up, mlp_w_down, ple_w_up, ple_w_gate, ple_norm)
    outs = []
    for x, p in ((x_prompt, p_prompt), (x_sample, p_sample)):
        b, s, d = x.shape
        y = _trunk(x.reshape(b * s, d), p.reshape(DEPTH, b * s, p.shape[-1]), b, s, w)
        outs.append(y.reshape(b, s, d))
    return tuple(outs)
```

```python
import functools

import numpy as np
import jax
import jax.numpy as jnp
from jax import lax
from jax.experimental import pallas as pl
from jax.experimental.pallas import tpu as pltpu

D_MODEL = 2048
DEPTH = 4
N_MIXERS = 2
MLA_HEADS = 16
Q_LORA = 512
KV_LORA = 512
NOPE_DIM = 128
ROPE_DIM = 64
V_DIM = 128
ROPE_THETA = 10000.0
SWA_Q_HEADS = 16
SWA_KV_HEADS = 4
SWA_GROUP = SWA_Q_HEADS // SWA_KV_HEADS
SWA_HEAD_DIM = 128
WINDOW = 128
BLOCK = 128
N_BUCKETS = 32
MAX_DISTANCE = 128
D_FF = 4 * D_MODEL
PLE_DIM = 256
EPS = 1e-6
NEG_INF = -1e30

LANES = 128
QK_PAD = 2 * LANES
ONES_COL = NOPE_DIM + ROPE_DIM
K2_COL = ONES_COL + 1
BOUND_MARGIN = 1.02
L_MIN = 2.0 ** -64
LOG2E = float(np.log2(np.e))
F32_ROWS = 8
BF16_ROWS = 16
VMEM_LIMIT = 56 * 1024 * 1024

F32 = jnp.float32
BF16 = jnp.bfloat16


def _rms(x, g):
    return x * lax.rsqrt(jnp.mean(x * x, axis=-1, keepdims=True) + EPS) * g


def _dot(a, b):
    return jnp.dot(a, b, preferred_element_type=F32)


def _const_spec(shape, **kw):
    nd = len(shape)
    return pl.BlockSpec(shape, lambda *_: (0,) * nd, **kw)


def _layer_spec(w, layer, **kw):
    return pl.BlockSpec((None,) + w.shape[1:], lambda *_: (layer, 0, 0), **kw)


def _params(sem):
    return pltpu.CompilerParams(dimension_semantics=sem, vmem_limit_bytes=VMEM_LIMIT)


def _swa_proj_kernel(x_ref, g_ref, w_ref, qt_ref, k_ref, vt_ref, *, q_scale):
    h = _rms(x_ref[...], g_ref[...]).astype(BF16)
    dh = SWA_HEAD_DIM
    q_w = SWA_Q_HEADS * dh
    kv_w = SWA_KV_HEADS * dh
    for c in range(0, q_w, kv_w):
        y = _dot(h, w_ref[:, c:c + kv_w]) * q_scale
        for hq in range(kv_w // dh):
            qt_ref[c + hq * dh:c + (hq + 1) * dh, :] = y[:, hq * dh:(hq + 1) * dh].T.astype(BF16)
    k_ref[...] = _dot(h, w_ref[:, q_w:q_w + kv_w]).astype(BF16)
    v = _dot(h, w_ref[:, q_w + kv_w:])
    for kh in range(SWA_KV_HEADS):
        vt_ref[kh * dh:(kh + 1) * dh, :] = v[:, kh * dh:(kh + 1) * dh].T.astype(BF16)


def _swa_proj(x, g, w, layer, tm=512):
    t, d = x.shape
    q_w = SWA_Q_HEADS * SWA_HEAD_DIM
    kv_w = SWA_KV_HEADS * SWA_HEAD_DIM
    return pl.pallas_call(
        functools.partial(_swa_proj_kernel, q_scale=float(SWA_HEAD_DIM ** -0.5) * LOG2E),
        grid=(t // tm,),
        in_specs=[pl.BlockSpec((tm, d), lambda i: (i, 0)), _const_spec((1, d)), _layer_spec(w, layer)],
        out_specs=[
            pl.BlockSpec((q_w, tm), lambda i: (0, i)),
            pl.BlockSpec((tm, kv_w), lambda i: (i, 0)),
            pl.BlockSpec((kv_w, tm), lambda i: (0, i)),
        ],
        out_shape=[
            jax.ShapeDtypeStruct((q_w, t), BF16),
            jax.ShapeDtypeStruct((t, kv_w), BF16),
            jax.ShapeDtypeStruct((kv_w, t), BF16),
        ],
        compiler_params=_params(("parallel",)),
        name="swa_proj",
    )(x, g, w)


def _mla_proj_kernel(x_ref, g_ref, wd_ref, qn_ref, kvn_ref, wuq_ref, wukv_ref, cos_ref, sin_ref,
                     q_ref, k_ref, v_ref, *, scale):
    h = _rms(x_ref[...], g_ref[...]).astype(BF16)
    lat = _dot(h, wd_ref[...])
    cq = _rms(lat[:, :Q_LORA], qn_ref[...]).astype(BF16)
    ckv = _rms(lat[:, Q_LORA:Q_LORA + KV_LORA], kvn_ref[...]).astype(BF16)
    cos = cos_ref[...]
    sin = sin_ref[...]
    base = Q_LORA + KV_LORA
    kr = lat[:, base:base + LANES] * cos + lat[:, base + LANES:base + 2 * LANES] * sin
    kr2 = jnp.sum(kr * kr, axis=1, keepdims=True)
    lane = lax.broadcasted_iota(jnp.int32, kr.shape, 1)
    hn = MLA_HEADS * NOPE_DIM
    grp = 4
    rw = grp * ROPE_DIM
    hr = MLA_HEADS * ROPE_DIM
    cos_g = jnp.tile(cos, (1, rw // LANES))
    sin_g = jnp.tile(sin, (1, rw // LANES))
    zero_rows = jnp.zeros((QK_PAD - NOPE_DIM - ROPE_DIM, kr.shape[0]), BF16)
    for h0 in range(0, MLA_HEADS, grp):
        lo = h0 * LANES
        w = grp * LANES
        ro = hn + h0 * ROPE_DIM
        qn = _dot(cq, wuq_ref[:, lo:lo + w]) * scale
        qr = _dot(cq, wuq_ref[:, ro:ro + rw])
        qs = _dot(cq, wuq_ref[:, hr + ro:hr + ro + rw])
        qrot = (qr * cos_g + qs * sin_g) * scale
        kv = _dot(ckv, wukv_ref[:, 2 * lo:2 * lo + 2 * w])
        for g in range(grp):
            hd = h0 + g
            c = g * LANES
            q_ref[hd, :NOPE_DIM, :] = qn[:, c:c + LANES].T.astype(BF16)
            pair_t = qrot[:, (g // 2) * LANES:(g // 2 + 1) * LANES].T
            q_ref[hd, NOPE_DIM:ONES_COL, :] = pair_t[(g % 2) * ROPE_DIM:(g % 2 + 1) * ROPE_DIM].astype(BF16)
            q_ref[hd, ONES_COL:, :] = zero_rows
            kn = kv[:, 2 * c:2 * c + LANES]
            k_ref[hd, :, :LANES] = kn.astype(BF16)
            k2 = (jnp.sum(kn * kn, axis=1, keepdims=True) + kr2) * BOUND_MARGIN
            k_hi = jnp.where(lane == K2_COL - LANES, k2, kr)
            k_ref[hd, :, LANES:] = jnp.where(lane == ONES_COL - LANES, 1.0, k_hi).astype(BF16)
            v_ref[hd] = kv[:, 2 * c + LANES:2 * c + 2 * LANES].T.astype(BF16)


def _mla_proj(x, g, wd, qn, kvn, wuq, wukv, layer, cos, sin, seq, tm=512):
    t, d = x.shape
    nblk = seq // tm
    scale = float((NOPE_DIM + ROPE_DIM) ** -0.5 * np.log2(np.e))
    heads = MLA_HEADS
    once = pl.Buffered(1)
    return pl.pallas_call(
        functools.partial(_mla_proj_kernel, scale=scale),
        grid=(t // tm,),
        in_specs=[
            pl.BlockSpec((tm, d), lambda i: (i, 0)),
            _const_spec((1, d)),
            _const_spec(wd.shape, pipeline_mode=once),
            _const_spec((1, Q_LORA)),
            _const_spec((1, KV_LORA)),
            _const_spec(wuq.shape, pipeline_mode=once),
            _layer_spec(wukv, layer, pipeline_mode=once),
            pl.BlockSpec((tm, LANES), lambda i: (i % nblk, 0)),
            pl.BlockSpec((tm, LANES), lambda i: (i % nblk, 0)),
        ],
        out_specs=[
            pl.BlockSpec((heads, QK_PAD, tm), lambda i: (0, 0, i)),
            pl.BlockSpec((heads, tm, QK_PAD), lambda i: (0, i, 0)),
            pl.BlockSpec((heads, V_DIM, tm), lambda i: (0, 0, i)),
        ],
        out_shape=[
            jax.ShapeDtypeStruct((heads, QK_PAD, t), BF16),
            jax.ShapeDtypeStruct((heads, t, QK_PAD), BF16),
            jax.ShapeDtypeStruct((heads, V_DIM, t), BF16),
        ],
        compiler_params=_params(("parallel",)),
        name="mla_proj",
    )(x, g, wd, qn, kvn, wuq, wukv, cos, sin)


def _mla_attn_fixed_shift_kernel(qt_ref, k_ref, vt_ref, o_ref, l_ref, acc_sc, k2_sc, *, tkc, group):
    seq = k_ref.shape[0]
    qt = qt_ref[...]
    tq = qt.shape[1]

    @pl.when(pl.program_id(2) == 0)
    def _():
        colmax = jnp.max(k_ref[:, LANES:].astype(F32), axis=0, keepdims=True)
        lane = lax.broadcasted_iota(jnp.int32, colmax.shape, 1)
        k2max = jnp.max(jnp.where(lane == K2_COL - LANES, colmax, 0.0), axis=1, keepdims=True)
        k2_sc[...] = jnp.broadcast_to(k2max, k2_sc.shape)

    qf = qt.astype(F32)
    q2 = jnp.sum(qf * qf, axis=0, keepdims=True)
    shift = -(jnp.sqrt(q2 * jnp.tile(k2_sc[0:1, :], (1, tq // LANES))) * BOUND_MARGIN)
    row = lax.broadcasted_iota(jnp.int32, qt.shape, 0)
    q_aug = jnp.where(row == ONES_COL, shift.astype(BF16), qt)

    def chunk(c):
        off = pl.multiple_of(c * tkc, tkc)
        p = jnp.exp2(_dot(k_ref[pl.ds(off, tkc), :], q_aug))
        return _dot(vt_ref[:, pl.ds(off, tkc)], p.astype(BF16)), jnp.sum(p, axis=0, keepdims=True)

    def body(j, l_run):
        upd, l_new = chunk(group * j)
        for g in range(1, group):
            u, l = chunk(group * j + g)
            upd = upd + u
            l_new = l_new + l
        acc_sc[...] += upd
        return l_run + l_new

    acc_sc[...] = jnp.zeros(acc_sc.shape, F32)
    l = lax.fori_loop(0, seq // (group * tkc), body, jnp.zeros((1, tq), F32))
    l_ref[...] = l
    o_ref[...] = (acc_sc[...] * (1.0 / l)).T.astype(o_ref.dtype)


def _mla_attn_fixed_shift(qt, k, vt, batch, seq, tq=2048, tkc=1024, group=2):
    heads, _, t = qt.shape
    nq = seq // tq
    return pl.pallas_call(
        functools.partial(_mla_attn_fixed_shift_kernel, tkc=tkc, group=group),
        grid=(batch, heads, nq),
        in_specs=[
            pl.BlockSpec((None, QK_PAD, tq), lambda b, h, i: (h, 0, b * nq + i)),
            pl.BlockSpec((None, seq, QK_PAD), lambda b, h, i: (h, b, 0)),
            pl.BlockSpec((None, V_DIM, seq), lambda b, h, i: (h, 0, b)),
        ],
        out_specs=[
            pl.BlockSpec((tq, V_DIM), lambda b, h, i: (b * nq + i, h)),
            pl.BlockSpec((None, 1, tq), lambda b, h, i: (h, 0, b * nq + i)),
        ],
        out_shape=[jax.ShapeDtypeStruct((t, heads * V_DIM), BF16), jax.ShapeDtypeStruct((heads, 1, t), F32)],
        scratch_shapes=[pltpu.VMEM((V_DIM, tq), F32), pltpu.VMEM((8, LANES), F32)],
        compiler_params=_params(("parallel", "parallel", "arbitrary")),
        name="mla_attn_fixed_shift",
    )(qt, k, vt)


def _mla_attn_online_max_kernel(qt_ref, k_ref, vt_ref, o_ref, sa_sc, sb_sc, acc_sc, *, tkc):
    seq = k_ref.shape[0]
    npairs = seq // (2 * tkc)
    qt = qt_ref[...]
    tq = qt.shape[1]

    def scores(c, s_sc):
        off = pl.multiple_of(c * tkc, tkc)
        s = _dot(k_ref[pl.ds(off, tkc), :], qt)
        s_sc[...] = s
        return jnp.max(s, axis=0, keepdims=True)

    def accumulate(c, s_sc, m_run, l_run, m_chunk):
        off = pl.multiple_of(c * tkc, tkc)
        m_new = jnp.maximum(m_run, m_chunk)
        alpha = jnp.exp2(m_run - m_new)
        p = jnp.exp2(s_sc[...] - m_new)
        l_new = alpha * l_run + jnp.sum(p, axis=0, keepdims=True)
        pv = _dot(vt_ref[:, pl.ds(off, tkc)], p.astype(BF16))
        acc_sc[...] = alpha * acc_sc[...] + pv
        return m_new, l_new

    def pair(j, carry, last):
        m_run, l_run, m_a = carry
        m_b = scores(2 * j + 1, sb_sc)
        m_run, l_run = accumulate(2 * j, sa_sc, m_run, l_run, m_a)
        if not last:
            m_a = scores(2 * j + 2, sa_sc)
        m_run, l_run = accumulate(2 * j + 1, sb_sc, m_run, l_run, m_b)
        return m_run, l_run, m_a

    acc_sc[...] = jnp.zeros(acc_sc.shape, F32)
    init = (jnp.full((1, tq), -jnp.inf, F32), jnp.zeros((1, tq), F32), scores(0, sa_sc))
    carry = lax.fori_loop(0, npairs - 1, functools.partial(pair, last=False), init)
    _, l_run, _ = pair(npairs - 1, carry, last=True)
    o_ref[...] = (acc_sc[...] * (1.0 / l_run)).T.astype(o_ref.dtype)


def _mla_attn_online_max(qt, k, vt, batch, seq, tq=1024, tkc=1024):
    heads, _, t = qt.shape
    nq = seq // tq
    return pl.pallas_call(
        functools.partial(_mla_attn_online_max_kernel, tkc=tkc),
        grid=(batch, heads, nq),
        in_specs=[
            pl.BlockSpec((None, QK_PAD, tq), lambda b, h, i: (h, 0, b * nq + i)),
            pl.BlockSpec((None, seq, QK_PAD), lambda b, h, i: (h, b, 0)),
            pl.BlockSpec((None, V_DIM, seq), lambda b, h, i: (h, 0, b)),
        ],
        out_specs=pl.BlockSpec((tq, V_DIM), lambda b, h, i: (b * nq + i, h)),
        out_shape=jax.ShapeDtypeStruct((t, heads * V_DIM), BF16),
        scratch_shapes=[
            pltpu.VMEM((tkc, tq), F32),
            pltpu.VMEM((tkc, tq), F32),
            pltpu.VMEM((V_DIM, tq), F32),
        ],
        compiler_params=_params(("parallel", "parallel", "arbitrary")),
        name="mla_attn_online_max",
    )(qt, k, vt)


def _mla_attn(qt, k, vt, batch, seq):
    o, l = _mla_attn_fixed_shift(qt, k, vt, batch, seq)
    return lax.cond(jnp.min(l) >= L_MIN, lambda: o, lambda: _mla_attn_online_max(qt, k, vt, batch, seq))


def _t5_bucket(rel):
    nb = N_BUCKETS // 2
    max_exact = nb // 2
    ret = (rel > 0).astype(np.int32) * nb
    n = np.abs(rel)
    large = max_exact + (np.log(np.maximum(n, 1).astype(np.float32) / max_exact)
                         / np.log(MAX_DISTANCE / max_exact) * (nb - max_exact)).astype(np.int32)
    large = np.minimum(large, nb - 1)
    return (ret + np.where(n < max_exact, n, large)).astype(np.int32)


def _bias_table_kernel(rb_ref, bucket_ref, o_ref):
    hd = pl.program_id(0)
    bucket = bucket_ref[...]
    acc = jnp.zeros(bucket.shape, F32)
    for b in range(N_BUCKETS):
        acc = jnp.where(bucket == b, rb_ref[b, hd], acc)
    si = lax.broadcasted_iota(jnp.int32, bucket.shape, 0)
    qi = lax.broadcasted_iota(jnp.int32, bucket.shape, 1)
    o_ref[...] = jnp.where(jnp.abs(si - BLOCK - qi) <= WINDOW, acc * LOG2E, NEG_INF)


def _bias_table(rel_bias):
    si = np.arange(3 * BLOCK)[:, None]
    qi = np.arange(BLOCK)[None, :]
    bucket = jnp.asarray(_t5_bucket(si - BLOCK - qi))
    return pl.pallas_call(
        _bias_table_kernel,
        grid=(SWA_Q_HEADS,),
        in_specs=[pl.BlockSpec(memory_space=pltpu.SMEM), _const_spec((3 * BLOCK, BLOCK))],
        out_specs=pl.BlockSpec((None, 3 * BLOCK, BLOCK), lambda h: (h, 0, 0)),
        out_shape=jax.ShapeDtypeStruct((SWA_Q_HEADS, 3 * BLOCK, BLOCK), F32),
        name="t5_bias_table",
    )(rel_bias, bucket)


def _swa_attn_kernel(sink_ref, qt_ref, kp_ref, kc_ref, kn_ref, vp_ref, vc_ref, vn_ref, bias_ref, ot_ref,
                     sa_sc, sb_sc, *, nqb, nblocks):
    i = pl.program_id(1)
    dh = SWA_HEAD_DIM
    kband = jnp.concatenate([kp_ref[...], kc_ref[...], kn_ref[...]], axis=0)
    vband = jnp.concatenate([vp_ref[...], vc_ref[...], vn_ref[...]], axis=1)
    units = [(j, kh) for j in range(nqb) for kh in range(SWA_KV_HEADS)]
    bufs = (sa_sc, sb_sc)

    def scores(unit, s_sc):
        j, kh = unit
        r0 = j * BLOCK
        kb = kband[r0:r0 + 3 * BLOCK, kh * dh:(kh + 1) * dh]
        heads = range(kh * SWA_GROUP, (kh + 1) * SWA_GROUP)
        qt = jnp.concatenate([qt_ref[hq * dh:(hq + 1) * dh, r0:r0 + BLOCK] for hq in heads], axis=1)
        s_sc[...] = _dot(kb, qt)

    def run(at_sequence_edge):
        si = lax.broadcasted_iota(jnp.int32, (3 * BLOCK, 1), 0)
        scores(units[0], bufs[0])
        for u, (j, kh) in enumerate(units):
            if u + 1 < len(units):
                scores(units[u + 1], bufs[(u + 1) % 2])
            r0 = j * BLOCK
            vbt = vband[kh * dh:(kh + 1) * dh, r0:r0 + 3 * BLOCK]
            heads = range(kh * SWA_GROUP, (kh + 1) * SWA_GROUP)
            bias = jnp.concatenate([bias_ref[hq] for hq in heads], axis=1)
            s = bufs[u % 2][...] + bias
            if at_sequence_edge:
                blk = i * nqb + j
                lo_ok = jnp.logical_or(si >= BLOCK, blk > 0)
                hi_ok = jnp.logical_or(si < 2 * BLOCK, blk < nblocks - 1)
                s = s + jnp.where(lo_ok & hi_ok, 0.0, NEG_INF)
            sk = jnp.concatenate([jnp.full((1, BLOCK), sink_ref[hq] * LOG2E, F32) for hq in heads], axis=1)
            m = jnp.maximum(jnp.max(s, axis=0, keepdims=True), sk)
            e = jnp.exp2(s - m)
            denom = jnp.sum(e, axis=0, keepdims=True) + jnp.exp2(sk - m)
            ot = _dot(vbt, e.astype(BF16)) * (1.0 / denom)
            for g, hq in enumerate(heads):
                ot_ref[hq * dh:(hq + 1) * dh, r0:r0 + BLOCK] = ot[:, g * BLOCK:(g + 1) * BLOCK].astype(ot_ref.dtype)

    edge = jnp.logical_or(i == 0, i == pl.num_programs(1) - 1)
    pl.when(edge)(functools.partial(run, True))
    pl.when(jnp.logical_not(edge))(functools.partial(run, False))


def _swa_attn(qt, k, vt, sink, bias, batch, seq, tb=512):
    hq_w, t = qt.shape
    kv_w = k.shape[1]
    nqb = tb // BLOCK
    nsteps = seq // tb
    nblocks = seq // BLOCK

    def prev_blk(b, i):
        return b * nblocks + jnp.maximum(i * nqb - 1, 0)

    def next_blk(b, i):
        return b * nblocks + jnp.minimum((i + 1) * nqb, nblocks - 1)

    return pl.pallas_call(
        functools.partial(_swa_attn_kernel, nqb=nqb, nblocks=nblocks),
        grid=(batch, nsteps),
        in_specs=[
            pl.BlockSpec(memory_space=pltpu.SMEM),
            pl.BlockSpec((hq_w, tb), lambda b, i: (0, b * nsteps + i)),
            pl.BlockSpec((BLOCK, kv_w), lambda b, i: (prev_blk(b, i), 0)),
            pl.BlockSpec((tb, kv_w), lambda b, i: (b * nsteps + i, 0)),
            pl.BlockSpec((BLOCK, kv_w), lambda b, i: (next_blk(b, i), 0)),
            pl.BlockSpec((kv_w, BLOCK), lambda b, i: (0, prev_blk(b, i))),
            pl.BlockSpec((kv_w, tb), lambda b, i: (0, b * nsteps + i)),
            pl.BlockSpec((kv_w, BLOCK), lambda b, i: (0, next_blk(b, i))),
            _const_spec(bias.shape),
        ],
        out_specs=pl.BlockSpec((hq_w, tb), lambda b, i: (0, b * nsteps + i)),
        out_shape=jax.ShapeDtypeStruct((hq_w, t), BF16),
        scratch_shapes=[pltpu.VMEM((3 * BLOCK, SWA_GROUP * BLOCK), F32)] * 2,
        compiler_params=_params(("parallel", "parallel")),
        name="swa_attn",
    )(sink, qt, k, k, k, vt, vt, vt, bias)


def _out_proj_kernel(o_ref, w_ref, g_ref, x_ref, y_ref, *, feature_major):
    if feature_major:
        y = lax.dot_general(o_ref[...], w_ref[...], (((0,), (0,)), ((), ())), preferred_element_type=F32)
    else:
        y = _dot(o_ref[...], w_ref[...])
    y_ref[...] = x_ref[...] + _rms(y, g_ref[...])


def _out_proj_residual(o, w, layer, g, x, feature_major=False, tm=512):
    t, d = x.shape
    if feature_major:
        o_spec = pl.BlockSpec((o.shape[0], tm), lambda i: (0, i))
    else:
        o_spec = pl.BlockSpec((tm, o.shape[1]), lambda i: (i, 0))
    return pl.pallas_call(
        functools.partial(_out_proj_kernel, feature_major=feature_major),
        grid=(t // tm,),
        in_specs=[
            o_spec,
            _layer_spec(w, layer),
            _const_spec((1, d)),
            pl.BlockSpec((tm, d), lambda i: (i, 0)),
        ],
        out_specs=pl.BlockSpec((tm, d), lambda i: (i, 0)),
        out_shape=jax.ShapeDtypeStruct((t, d), F32),
        compiler_params=_params(("parallel",)),
        name="out_proj_residual",
    )(o, w, g, x)


def _mlp_kernel(x_ref, gin_ref, wup_ref, wdn_ref, gout_ref, y_ref, h_sc, acc_sc):
    j = pl.program_id(1)
    tm = x_ref.shape[0]

    @pl.when(j == 0)
    def _():
        g = gin_ref[...]
        for r in range(0, tm, BF16_ROWS):
            h_sc[r:r + BF16_ROWS, :] = _rms(x_ref[r:r + BF16_ROWS, :], g).astype(BF16)
        acc_sc[...] = jnp.zeros(acc_sc.shape, F32)

    u = jnp.maximum(_dot(h_sc[...], wup_ref[...]), 0.0)
    acc_sc[...] += _dot((u * u).astype(BF16), wdn_ref[...])

    @pl.when(j == pl.num_programs(1) - 1)
    def _():
        g = gout_ref[...]
        for r in range(0, tm, F32_ROWS):
            y_ref[r:r + F32_ROWS, :] = x_ref[r:r + F32_ROWS, :] + _rms(acc_sc[r:r + F32_ROWS, :], g)


def _mlp(x, gin, wup, wdn, gout, layer, tm=512, tf=1024):
    t, d = x.shape
    f = wup.shape[-1]
    return pl.pallas_call(
        _mlp_kernel,
        grid=(t // tm, f // tf),
        in_specs=[
            pl.BlockSpec((tm, d), lambda i, j: (i, 0)),
            _const_spec((1, d)),
            pl.BlockSpec((None, d, tf), lambda i, j: (layer, 0, j)),
            pl.BlockSpec((None, tf, d), lambda i, j: (layer, j, 0)),
            _const_spec((1, d)),
        ],
        out_specs=pl.BlockSpec((tm, d), lambda i, j: (i, 0)),
        out_shape=jax.ShapeDtypeStruct((t, d), F32),
        scratch_shapes=[pltpu.VMEM((tm, d), BF16), pltpu.VMEM((tm, d), F32)],
        compiler_params=_params(("parallel", "arbitrary")),
        name="mlp",
    )(x, gin, wup, wdn, gout)


def _ple_kernel(x_ref, p_ref, wpu_ref, pn_ref, wg_ref, y_ref):
    x = x_ref[...]
    e = _rms(_dot(p_ref[...].astype(BF16), wpu_ref[...]), pn_ref[...])
    z = _dot(x.astype(BF16), wg_ref[...])
    y_ref[...] = x + e / (1.0 + jnp.exp(-z))


def _ple(x, p, wpu, pn, wg, layer, tm=512):
    t, d = x.shape
    return pl.pallas_call(
        _ple_kernel,
        grid=(t // tm,),
        in_specs=[
            pl.BlockSpec((tm, d), lambda i: (i, 0)),
            pl.BlockSpec((None, tm, p.shape[-1]), lambda i: (layer, i, 0)),
            _layer_spec(wpu, layer),
            _const_spec((1, d)),
            _layer_spec(wg, layer),
        ],
        out_specs=pl.BlockSpec((tm, d), lambda i: (i, 0)),
        out_shape=jax.ShapeDtypeStruct((t, d), F32),
        compiler_params=_params(("parallel",)),
        name="ple",
    )(x, p, wpu, pn, wg)


def _pad_cols(w, width):
    return jnp.pad(w, ((0, 0), (0, width - w.shape[1])))


def _swap_halves(w):
    half = w.shape[1] // 2
    return jnp.concatenate([w[:, half:], w[:, :half]], axis=1)


def _prep_mla_weights(w_down, w_uq):
    base = Q_LORA + KV_LORA
    w_kr = w_down[:, base:]
    wd = jnp.concatenate([w_down[:, :base], _pad_cols(w_kr, LANES), _pad_cols(_swap_halves(w_kr), LANES)], axis=1)
    wq = w_uq.reshape(Q_LORA, MLA_HEADS, NOPE_DIM + ROPE_DIM)
    nope = wq[:, :, :NOPE_DIM].reshape(Q_LORA, MLA_HEADS * NOPE_DIM)
    rope = wq[:, :, NOPE_DIM:]
    half = ROPE_DIM // 2
    swapped = jnp.concatenate([rope[:, :, half:], rope[:, :, :half]], axis=2)
    rope = rope.reshape(Q_LORA, MLA_HEADS * ROPE_DIM)
    swapped = swapped.reshape(Q_LORA, MLA_HEADS * ROPE_DIM)
    return wd.astype(BF16), jnp.concatenate([nope, rope, swapped], axis=1).astype(BF16)


def _rope_tables(seq):
    half = ROPE_DIM // 2
    inv = 1.0 / (ROPE_THETA ** (jnp.arange(half, dtype=F32) / half))
    ang = jnp.arange(seq).astype(F32)[:, None] * inv[None, :]
    cos = jnp.cos(ang)
    sin = jnp.sin(ang)
    return jnp.tile(cos, (1, LANES // half)), jnp.tile(jnp.concatenate([-sin, sin], axis=1), (1, LANES // ROPE_DIM))


def _trunk(x, p, batch, seq, w):
    cos, sin = _rope_tables(seq)
    for i in range(DEPTH):
        g = w["norm_gains"][i]
        j = i // N_MIXERS
        if i % N_MIXERS == 0:
            q, k, v = _mla_proj(x, g[0:1], w["mla_w_down"][j], w["mla_q_norm"][j:j + 1], w["mla_kv_norm"][j:j + 1],
                                w["mla_w_uq"][j], w["mla_w_ukv"], j, cos, sin, seq)
            o = _mla_attn(q, k, v, batch, seq)
            x = _out_proj_residual(o, w["mla_w_o"], j, g[1:2], x)
        else:
            qt, k, vt = _swa_proj(x, g[0:1], w["swa_w_qkv"], j)
            ot = _swa_attn(qt, k, vt, w["swa_sink"][j], w["bias_table"], batch, seq)
            x = _out_proj_residual(ot, w["swa_w_o"], j, g[1:2], x, feature_major=True)
        x = _mlp(x, g[2:3], w["mlp_w_up"], w["mlp_w_down"], g[3:4], i)
        x = _ple(x, p, w["ple_w_up"], w["ple_norm"][i:i + 1], w["ple_w_gate"], i)
    return x


def _prep_weights(norm_gains, mla_w_down, mla_q_norm, mla_kv_norm, mla_w_uq, mla_w_ukv, mla_w_o, swa_w_qkv,
                  swa_sink, swa_w_o, rel_bias, mlp_w_up, mlp_w_down, ple_w_up, ple_w_gate, ple_norm):
    wd, wq = zip(*[_prep_mla_weights(mla_w_down[j], mla_w_uq[j]) for j in range(mla_w_down.shape[0])])
    return dict(
        norm_gains=norm_gains, mla_w_down=wd, mla_q_norm=mla_q_norm, mla_kv_norm=mla_kv_norm, mla_w_uq=wq,
        mla_w_ukv=mla_w_ukv.astype(BF16), mla_w_o=mla_w_o.astype(BF16), swa_w_qkv=swa_w_qkv.astype(BF16),
        swa_sink=swa_sink, swa_w_o=swa_w_o.astype(BF16), bias_table=_bias_table(rel_bias),
        mlp_w_up=mlp_w_up.astype(BF16), mlp_w_down=mlp_w_down.astype(BF16), ple_w_up=ple_w_up.astype(BF16),
        ple_w_gate=ple_w_gate.astype(BF16), ple_norm=ple_norm)


def kernel(x_prompt, x_sample, p_prompt, p_sample, norm_gains, mla_w_down, mla_q_norm, mla_kv_norm, mla_w_uq,
           mla_w_ukv, mla_w_o, swa_w_qkv, swa_sink, swa_w_o, rel_bias, mlp_w_up, mlp_w_down, ple_w_up, ple_w_gate,
           ple_norm):
    w = _prep_weights(norm_gains, mla_w_down, mla_q_norm, mla_kv_norm, mla_w_uq, mla_w_ukv, mla_w_o, swa_w_qkv,
                      swa_sink, swa_w_o, rel_bias, mlp_w_up, mlp_w_down, ple_w_up, ple_w_gate, ple_norm)
    outs = []
    for x, p in ((x_prompt, p_prompt), (x_sample, p_sample)):
        b, s, d = x.shape
        y = _trunk(x.reshape(b * s, d), p.reshape(DEPTH, b * s, p.shape[-1]), b, s, w)
        outs.append(y.reshape(b, s, d))
    return tuple(outs)
```

```python
import functools

import numpy as np
import jax
import jax.numpy as jnp
from jax import lax
from jax.experimental import pallas as pl
from jax.experimental.pallas import tpu as pltpu

D_MODEL = 2048
DEPTH = 4
N_MIXERS = 2
MLA_HEADS = 16
Q_LORA = 512
KV_LORA = 512
NOPE_DIM = 128
ROPE_DIM = 64
V_DIM = 128
ROPE_THETA = 10000.0
SWA_Q_HEADS = 16
SWA_KV_HEADS = 4
SWA_GROUP = SWA_Q_HEADS // SWA_KV_HEADS
SWA_HEAD_DIM = 128
WINDOW = 128
BLOCK = 128
N_BUCKETS = 32
MAX_DISTANCE = 128
D_FF = 4 * D_MODEL
PLE_DIM = 256
EPS = 1e-6
NEG_INF = -1e30

LANES = 128
QK_PAD = 2 * LANES
ONES_COL = NOPE_DIM + ROPE_DIM
K2_COL = ONES_COL + 1
BOUND_MARGIN = 1.02
L_MIN = 2.0 ** -64
LOG2E = float(np.log2(np.e))
F32_ROWS = 8
BF16_ROWS = 16
VMEM_LIMIT = 56 * 1024 * 1024

F32 = jnp.float32
BF16 = jnp.bfloat16


def _rms(x, g):
    return x * lax.rsqrt(jnp.mean(x * x, axis=-1, keepdims=True) + EPS) * g


def _dot(a, b):
    return jnp.dot(a, b, preferred_element_type=F32)


def _const_spec(shape, **kw):
    nd = len(shape)
    return pl.BlockSpec(shape, lambda *_: (0,) * nd, **kw)


def _layer_spec(w, layer, **kw):
    return pl.BlockSpec((None,) + w.shape[1:], lambda *_: (layer, 0, 0), **kw)


def _params(sem):
    return pltpu.CompilerParams(dimension_semantics=sem, vmem_limit_bytes=VMEM_LIMIT)


def _swa_proj_kernel(x_ref, g_ref, w_ref, qt_ref, k_ref, vt_ref, *, q_scale):
    h = _rms(x_ref[...], g_ref[...]).astype(BF16)
    dh = SWA_HEAD_DIM
    q_w = SWA_Q_HEADS * dh
    kv_w = SWA_KV_HEADS * dh
    for c in range(0, q_w, kv_w):
        y = _dot(h, w_ref[:, c:c + kv_w]) * q_scale
        for hq in range(kv_w // dh):
            qt_ref[c + hq * dh:c + (hq + 1) * dh, :] = y[:, hq * dh:(hq + 1) * dh].T.astype(BF16)
    k_ref[...] = _dot(h, w_ref[:, q_w:q_w + kv_w]).astype(BF16)
    v = _dot(h, w_ref[:, q_w + kv_w:])
    for kh in range(SWA_KV_HEADS):
        vt_ref[kh * dh:(kh + 1) * dh, :] = v[:, kh * dh:(kh + 1) * dh].T.astype(BF16)


def _swa_proj(x, g, w, layer, tm=512):
    t, d = x.shape
    q_w = SWA_Q_HEADS * SWA_HEAD_DIM
    kv_w = SWA_KV_HEADS * SWA_HEAD_DIM
    return pl.pallas_call(
        functools.partial(_swa_proj_kernel, q_scale=float(SWA_HEAD_DIM ** -0.5) * LOG2E),
        grid=(t // tm,),
        in_specs=[pl.BlockSpec((tm, d), lambda i: (i, 0)), _const_spec((1, d)), _layer_spec(w, layer)],
        out_specs=[
            pl.BlockSpec((q_w, tm), lambda i: (0, i)),
            pl.BlockSpec((tm, kv_w), lambda i: (i, 0)),
            pl.BlockSpec((kv_w, tm), lambda i: (0, i)),
        ],
        out_shape=[
            jax.ShapeDtypeStruct((q_w, t), BF16),
            jax.ShapeDtypeStruct((t, kv_w), BF16),
            jax.ShapeDtypeStruct((kv_w, t), BF16),
        ],
        compiler_params=_params(("parallel",)),
        name="swa_proj",
    )(x, g, w)


def _mla_proj_kernel(x_ref, g_ref, wd_ref, qn_ref, kvn_ref, wuq_ref, wukv_ref, cos_ref, sin_ref,
                     q_ref, k_ref, v_ref, *, scale):
    h = _rms(x_ref[...], g_ref[...]).astype(BF16)
    lat = _dot(h, wd_ref[...])
    cq = _rms(lat[:, :Q_LORA], qn_ref[...]).astype(BF16)
    ckv = _rms(lat[:, Q_LORA:Q_LORA + KV_LORA], kvn_ref[...]).astype(BF16)
    cos = cos_ref[...]
    sin = sin_ref[...]
    base = Q_LORA + KV_LORA
    kr = lat[:, base:base + LANES] * cos + lat[:, base + LANES:base + 2 * LANES] * sin
    kr2 = jnp.sum(kr * kr, axis=1, keepdims=True)
    lane = lax.broadcasted_iota(jnp.int32, kr.shape, 1)
    hn = MLA_HEADS * NOPE_DIM
    grp = 4
    rw = grp * ROPE_DIM
    hr = MLA_HEADS * ROPE_DIM
    cos_g = jnp.tile(cos, (1, rw // LANES))
    sin_g = jnp.tile(sin, (1, rw // LANES))
    zero_rows = jnp.zeros((QK_PAD - NOPE_DIM - ROPE_DIM, kr.shape[0]), BF16)
    for h0 in range(0, MLA_HEADS, grp):
        lo = h0 * LANES
        w = grp * LANES
        ro = hn + h0 * ROPE_DIM
        qn = _dot(cq, wuq_ref[:, lo:lo + w]) * scale
        qr = _dot(cq, wuq_ref[:, ro:ro + rw])
        qs = _dot(cq, wuq_ref[:, hr + ro:hr + ro + rw])
        qrot = (qr * cos_g + qs * sin_g) * scale
        kv = _dot(ckv, wukv_ref[:, 2 * lo:2 * lo + 2 * w])
        for g in range(grp):
            hd = h0 + g
            c = g * LANES
            q_ref[hd, :NOPE_DIM, :] = qn[:, c:c + LANES].T.astype(BF16)
            pair_t = qrot[:, (g // 2) * LANES:(g // 2 + 1) * LANES].T
            q_ref[hd, NOPE_DIM:ONES_COL, :] = pair_t[(g % 2) * ROPE_DIM:(g % 2 + 1) * ROPE_DIM].astype(BF16)
            q_ref[hd, ONES_COL:, :] = zero_rows
            kn = kv[:, 2 * c:2 * c + LANES]
            k_ref[hd, :, :LANES] = kn.astype(BF16)
            k2 = (jnp.sum(kn * kn, axis=1, keepdims=True) + kr2) * BOUND_MARGIN
            k_hi = jnp.where(lane == K2_COL - LANES, k2, kr)
            k_ref[hd, :, LANES:] = jnp.where(lane == ONES_COL - LANES, 1.0, k_hi).astype(BF16)
            v_ref[hd] = kv[:, 2 * c + LANES:2 * c + 2 * LANES].T.astype(BF16)


def _mla_proj(x, g, wd, qn, kvn, wuq, wukv, layer, cos, sin, seq, tm=512):
    t, d = x.shape
    nblk = seq // tm
    scale = float((NOPE_DIM + ROPE_DIM) ** -0.5 * np.log2(np.e))
    heads = MLA_HEADS
    once = pl.Buffered(1)
    return pl.pallas_call(
        functools.partial(_mla_proj_kernel, scale=scale),
        grid=(t // tm,),
        in_specs=[
            pl.BlockSpec((tm, d), lambda i: (i, 0)),
            _const_spec((1, d)),
            _const_spec(wd.shape, pipeline_mode=once),
            _const_spec((1, Q_LORA)),
            _const_spec((1, KV_LORA)),
            _const_spec(wuq.shape, pipeline_mode=once),
            _layer_spec(wukv, layer, pipeline_mode=once),
            pl.BlockSpec((tm, LANES), lambda i: (i % nblk, 0)),
            pl.BlockSpec((tm, LANES), lambda i: (i % nblk, 0)),
        ],
        out_specs=[
            pl.BlockSpec((heads, QK_PAD, tm), lambda i: (0, 0, i)),
            pl.BlockSpec((heads, tm, QK_PAD), lambda i: (0, i, 0)),
            pl.BlockSpec((heads, V_DIM, tm), lambda i: (0, 0, i)),
        ],
        out_shape=[
            jax.ShapeDtypeStruct((heads, QK_PAD, t), BF16),
            jax.ShapeDtypeStruct((heads, t, QK_PAD), BF16),
            jax.ShapeDtypeStruct((heads, V_DIM, t), BF16),
        ],
        compiler_params=_params(("parallel",)),
        name="mla_proj",
    )(x, g, wd, qn, kvn, wuq, wukv, cos, sin)


def _mla_attn_fixed_shift_kernel(qt_ref, k_ref, vt_ref, o_ref, l_ref, acc_sc, k2_sc, *, tkc, group):
    seq = k_ref.shape[0]
    qt = qt_ref[...]
    tq = qt.shape[1]

    @pl.when(pl.program_id(2) == 0)
    def _():
        colmax = jnp.max(k_ref[:, LANES:].astype(F32), axis=0, keepdims=True)
        lane = lax.broadcasted_iota(jnp.int32, colmax.shape, 1)
        k2max = jnp.max(jnp.where(lane == K2_COL - LANES, colmax, 0.0), axis=1, keepdims=True)
        k2_sc[...] = jnp.broadcast_to(k2max, k2_sc.shape)

    qf = qt.astype(F32)
    q2 = jnp.sum(qf * qf, axis=0, keepdims=True)
    shift = -(jnp.sqrt(q2 * jnp.tile(k2_sc[0:1, :], (1, tq // LANES))) * BOUND_MARGIN)
    row = lax.broadcasted_iota(jnp.int32, qt.shape, 0)
    q_aug = jnp.where(row == ONES_COL, shift.astype(BF16), qt)

    def chunk(c):
        off = pl.multiple_of(c * tkc, tkc)
        p = jnp.exp2(_dot(k_ref[pl.ds(off, tkc), :], q_aug))
        return _dot(vt_ref[:, pl.ds(off, tkc)], p.astype(BF16)), jnp.sum(p, axis=0, keepdims=True)

    def body(j, l_run):
        upd, l_new = chunk(group * j)
        for g in range(1, group):
            u, l = chunk(group * j + g)
            upd = upd + u
            l_new = l_new + l
        acc_sc[...] += upd
        return l_run + l_new

    acc_sc[...] = jnp.zeros(acc_sc.shape, F32)
    l = lax.fori_loop(0, seq // (group * tkc), body, jnp.zeros((1, tq), F32))
    l_ref[...] = l
    o_ref[...] = (acc_sc[...] * (1.0 / l)).T.astype(o_ref.dtype)


def _mla_attn_fixed_shift(qt, k, vt, batch, seq, tq=2048, tkc=1024, group=2):
    heads, _, t = qt.shape
    nq = seq // tq
    return pl.pallas_call(
        functools.partial(_mla_attn_fixed_shift_kernel, tkc=tkc, group=group),
        grid=(batch, heads, nq),
        in_specs=[
            pl.BlockSpec((None, QK_PAD, tq), lambda b, h, i: (h, 0, b * nq + i)),
            pl.BlockSpec((None, seq, QK_PAD), lambda b, h, i: (h, b, 0)),
            pl.BlockSpec((None, V_DIM, seq), lambda b, h, i: (h, 0, b)),
        ],
        out_specs=[
            pl.BlockSpec((tq, V_DIM), lambda b, h, i: (b * nq + i, h)),
            pl.BlockSpec((None, 1, tq), lambda b, h, i: (h, 0, b * nq + i)),
        ],
        out_shape=[jax.ShapeDtypeStruct((t, heads * V_DIM), BF16), jax.ShapeDtypeStruct((heads, 1, t), F32)],
        scratch_shapes=[pltpu.VMEM((V_DIM, tq), F32), pltpu.VMEM((8, LANES), F32)],
        compiler_params=_params(("parallel", "parallel", "arbitrary")),
        name="mla_attn_fixed_shift",
    )(qt, k, vt)


def _mla_attn_online_max_kernel(qt_ref, k_ref, vt_ref, o_ref, sa_sc, sb_sc, acc_sc, *, tkc):
    seq = k_ref.shape[0]
    npairs = seq // (2 * tkc)
    qt = qt_ref[...]
    tq = qt.shape[1]

    def scores(c, s_sc):
        off = pl.multiple_of(c * tkc, tkc)
        s = _dot(k_ref[pl.ds(off, tkc), :], qt)
        s_sc[...] = s
        return jnp.max(s, axis=0, keepdims=True)

    def accumulate(c, s_sc, m_run, l_run, m_chunk):
        off = pl.multiple_of(c * tkc, tkc)
        m_new = jnp.maximum(m_run, m_chunk)
        alpha = jnp.exp2(m_run - m_new)
        p = jnp.exp2(s_sc[...] - m_new)
        l_new = alpha * l_run + jnp.sum(p, axis=0, keepdims=True)
        pv = _dot(vt_ref[:, pl.ds(off, tkc)], p.astype(BF16))
        acc_sc[...] = alpha * acc_sc[...] + pv
        return m_new, l_new

    def pair(j, carry, last):
        m_run, l_run, m_a = carry
        m_b = scores(2 * j + 1, sb_sc)
        m_run, l_run = accumulate(2 * j, sa_sc, m_run, l_run, m_a)
        if not last:
            m_a = scores(2 * j + 2, sa_sc)
        m_run, l_run = accumulate(2 * j + 1, sb_sc, m_run, l_run, m_b)
        return m_run, l_run, m_a

    acc_sc[...] = jnp.zeros(acc_sc.shape, F32)
    init = (jnp.full((1, tq), -jnp.inf, F32), jnp.zeros((1, tq), F32), scores(0, sa_sc))
    carry = lax.fori_loop(0, npairs - 1, functools.partial(pair, last=False), init)
    _, l_run, _ = pair(npairs - 1, carry, last=True)
    o_ref[...] = (acc_sc[...] * (1.0 / l_run)).T.astype(o_ref.dtype)


def _mla_attn_online_max(qt, k, vt, batch, seq, tq=1024, tkc=1024):
    heads, _, t = qt.shape
    nq = seq // tq
    return pl.pallas_call(
        functools.partial(_mla_attn_online_max_kernel, tkc=tkc),
        grid=(batch, heads, nq),
        in_specs=[
            pl.BlockSpec((None, QK_PAD, tq), lambda b, h, i: (h, 0, b * nq + i)),
            pl.BlockSpec((None, seq, QK_PAD), lambda b, h, i: (h, b, 0)),
            pl.BlockSpec((None, V_DIM, seq), lambda b, h, i: (h, 0, b)),
        ],
        out_specs=pl.BlockSpec((tq, V_DIM), lambda b, h, i: (b * nq + i, h)),
        out_shape=jax.ShapeDtypeStruct((t, heads * V_DIM), BF16),
        scratch_shapes=[
            pltpu.VMEM((tkc, tq), F32),
            pltpu.VMEM((tkc, tq), F32),
            pltpu.VMEM((V_DIM, tq), F32),
        ],
        compiler_params=_params(("parallel", "parallel", "arbitrary")),
        name="mla_attn_online_max",
    )(qt, k, vt)


def _mla_attn(qt, k, vt, batch, seq):
    o, l = _mla_attn_fixed_shift(qt, k, vt, batch, seq)
    return lax.cond(jnp.min(l) >= L_MIN, lambda: o, lambda: _mla_attn_online_max(qt, k, vt, batch, seq))


def _t5_bucket(rel):
    nb = N_BUCKETS // 2
    max_exact = nb // 2
    ret = (rel > 0).astype(np.int32) * nb
    n = np.abs(rel)
    large = max_exact + (np.log(np.maximum(n, 1).astype(np.float32) / max_exact)
                         / np.log(MAX_DISTANCE / max_exact) * (nb - max_exact)).astype(np.int32)
    large = np.minimum(large, nb - 1)
    return (ret + np.where(n < max_exact, n, large)).astype(np.int32)


def _bias_table_kernel(rb_ref, bucket_ref, o_ref):
    hd = pl.program_id(0)
    bucket = bucket_ref[...]
    acc = jnp.zeros(bucket.shape, F32)
    for b in range(N_BUCKETS):
        acc = jnp.where(bucket == b, rb_ref[b, hd], acc)
    si = lax.broadcasted_iota(jnp.int32, bucket.shape, 0)
    qi = lax.broadcasted_iota(jnp.int32, bucket.shape, 1)
    o_ref[...] = jnp.where(jnp.abs(si - BLOCK - qi) <= WINDOW, acc * LOG2E, NEG_INF)


def _bias_table(rel_bias):
    si = np.arange(3 * BLOCK)[:, None]
    qi = np.arange(BLOCK)[None, :]
    bucket = jnp.asarray(_t5_bucket(si - BLOCK - qi))
    return pl.pallas_call(
        _bias_table_kernel,
        grid=(SWA_Q_HEADS,),
        in_specs=[pl.BlockSpec(memory_space=pltpu.SMEM), _const_spec((3 * BLOCK, BLOCK))],
        out_specs=pl.BlockSpec((None, 3 * BLOCK, BLOCK), lambda h: (h, 0, 0)),
        out_shape=jax.ShapeDtypeStruct((SWA_Q_HEADS, 3 * BLOCK, BLOCK), F32),
        name="t5_bias_table",
    )(rel_bias, bucket)


def _swa_attn_kernel(sink_ref, qt_ref, kp_ref, kc_ref, kn_ref, vp_ref, vc_ref, vn_ref, bias_ref, ot_ref,
                     sa_sc, sb_sc, *, nqb, nblocks):
    i = pl.program_id(1)
    dh = SWA_HEAD_DIM
    kband = jnp.concatenate([kp_ref[...], kc_ref[...], kn_ref[...]], axis=0)
    vband = jnp.concatenate([vp_ref[...], vc_ref[...], vn_ref[...]], axis=1)
    units = [(j, kh) for j in range(nqb) for kh in range(SWA_KV_HEADS)]
    bufs = (sa_sc, sb_sc)

    def scores(unit, s_sc):
        j, kh = unit
        r0 = j * BLOCK
        kb = kband[r0:r0 + 3 * BLOCK, kh * dh:(kh + 1) * dh]
        heads = range(kh * SWA_GROUP, (kh + 1) * SWA_GROUP)
        qt = jnp.concatenate([qt_ref[hq * dh:(hq + 1) * dh, r0:r0 + BLOCK] for hq in heads], axis=1)
        s_sc[...] = _dot(kb, qt)

    def run(at_sequence_edge):
        si = lax.broadcasted_iota(jnp.int32, (3 * BLOCK, 1), 0)
        scores(units[0], bufs[0])
        for u, (j, kh) in enumerate(units):
            if u + 1 < len(units):
                scores(units[u + 1], bufs[(u + 1) % 2])
            r0 = j * BLOCK
            vbt = vband[kh * dh:(kh + 1) * dh, r0:r0 + 3 * BLOCK]
            heads = range(kh * SWA_GROUP, (kh + 1) * SWA_GROUP)
            bias = jnp.concatenate([bias_ref[hq] for hq in heads], axis=1)
            s = bufs[u % 2][...] + bias
            if at_sequence_edge:
                blk = i * nqb + j
                lo_ok = jnp.logical_or(si >= BLOCK, blk > 0)
                hi_ok = jnp.logical_or(si < 2 * BLOCK, blk < nblocks - 1)
                s = s + jnp.where(lo_ok & hi_ok, 0.0, NEG_INF)
            sk = jnp.concatenate([jnp.full((1, BLOCK), sink_ref[hq] * LOG2E, F32) for hq in heads], axis=1)
            m = jnp.maximum(jnp.max(s, axis=0, keepdims=True), sk)
            e = jnp.exp2(s - m)
            denom = jnp.sum(e, axis=0, keepdims=True) + jnp.exp2(sk - m)
            ot = _dot(vbt, e.astype(BF16)) * (1.0 / denom)
            for g, hq in enumerate(heads):
                ot_ref[hq * dh:(hq + 1) * dh, r0:r0 + BLOCK] = ot[:, g * BLOCK:(g + 1) * BLOCK].astype(ot_ref.dtype)

    edge = jnp.logical_or(i == 0, i == pl.num_programs(1) - 1)
    pl.when(edge)(functools.partial(run, True))
    pl.when(jnp.logical_not(edge))(functools.partial(run, False))


def _swa_attn(qt, k, vt, sink, bias, batch, seq, tb=512):
    hq_w, t = qt.shape
    kv_w = k.shape[1]
    nqb = tb // BLOCK
    nsteps = seq // tb
    nblocks = seq // BLOCK

    def prev_blk(b, i):
        return b * nblocks + jnp.maximum(i * nqb - 1, 0)

    def next_blk(b, i):
        return b * nblocks + jnp.minimum((i + 1) * nqb, nblocks - 1)

    return pl.pallas_call(
        functools.partial(_swa_attn_kernel, nqb=nqb, nblocks=nblocks),
        grid=(batch, nsteps),
        in_specs=[
            pl.BlockSpec(memory_space=pltpu.SMEM),
            pl.BlockSpec((hq_w, tb), lambda b, i: (0, b * nsteps + i)),
            pl.BlockSpec((BLOCK, kv_w), lambda b, i: (prev_blk(b, i), 0)),
            pl.BlockSpec((tb, kv_w), lambda b, i: (b * nsteps + i, 0)),
            pl.BlockSpec((BLOCK, kv_w), lambda b, i: (next_blk(b, i), 0)),
            pl.BlockSpec((kv_w, BLOCK), lambda b, i: (0, prev_blk(b, i))),
            pl.BlockSpec((kv_w, tb), lambda b, i: (0, b * nsteps + i)),
            pl.BlockSpec((kv_w, BLOCK), lambda b, i: (0, next_blk(b, i))),
            _const_spec(bias.shape),
        ],
        out_specs=pl.BlockSpec((hq_w, tb), lambda b, i: (0, b * nsteps + i)),
        out_shape=jax.ShapeDtypeStruct((hq_w, t), BF16),
        scratch_shapes=[pltpu.VMEM((3 * BLOCK, SWA_GROUP * BLOCK), F32)] * 2,
        compiler_params=_params(("parallel", "parallel")),
        name="swa_attn",
    )(sink, qt, k, k, k, vt, vt, vt, bias)


def _out_proj_kernel(o_ref, w_ref, g_ref, x_ref, y_ref, *, feature_major):
    if feature_major:
        y = lax.dot_general(o_ref[...], w_ref[...], (((0,), (0,)), ((), ())), preferred_element_type=F32)
    else:
        y = _dot(o_ref[...], w_ref[...])
    y_ref[...] = x_ref[...] + _rms(y, g_ref[...])


def _out_proj_residual(o, w, layer, g, x, feature_major=False, tm=512):
    t, d = x.shape
    if feature_major:
        o_spec = pl.BlockSpec((o.shape[0], tm), lambda i: (0, i))
    else:
        o_spec = pl.BlockSpec((tm, o.shape[1]), lambda i: (i, 0))
    return pl.pallas_call(
        functools.partial(_out_proj_kernel, feature_major=feature_major),
        grid=(t // tm,),
        in_specs=[
            o_spec,
            _layer_spec(w, layer),
            _const_spec((1, d)),
            pl.BlockSpec((tm, d), lambda i: (i, 0)),
        ],
        out_specs=pl.BlockSpec((tm, d), lambda i: (i, 0)),
        out_shape=jax.ShapeDtypeStruct((t, d), F32),
        compiler_params=_params(("parallel",)),
        name="out_proj_residual",
    )(o, w, g, x)


def _mlp_kernel(x_ref, gin_ref, wup_ref, wdn_ref, gout_ref, y_ref, h_sc, acc_sc):
    j = pl.program_id(1)
    tm = x_ref.shape[0]

    @pl.when(j == 0)
    def _():
        g = gin_ref[...]
        for r in range(0, tm, BF16_ROWS):
            h_sc[r:r + BF16_ROWS, :] = _rms(x_ref[r:r + BF16_ROWS, :], g).astype(BF16)
        acc_sc[...] = jnp.zeros(acc_sc.shape, F32)

    u = jnp.maximum(_dot(h_sc[...], wup_ref[...]), 0.0)
    acc_sc[...] += _dot((u * u).astype(BF16), wdn_ref[...])

    @pl.when(j == pl.num_programs(1) - 1)
    def _():
        g = gout_ref[...]
        for r in range(0, tm, F32_ROWS):
            y_ref[r:r + F32_ROWS, :] = x_ref[r:r + F32_ROWS, :] + _rms(acc_sc[r:r + F32_ROWS, :], g)


def _mlp(x, gin, wup, wdn, gout, layer, tm=512):
    t, d = x.shape
    nf, tf = wup.shape[1], wup.shape[3]
    return pl.pallas_call(
        _mlp_kernel,
        grid=(t // tm, nf),
        in_specs=[
            pl.BlockSpec((tm, d), lambda i, j: (i, 0)),
            _const_spec((1, d)),
            pl.BlockSpec((None, None, d, tf), lambda i, j: (layer, j, 0, 0)),
            pl.BlockSpec((None, tf, d), lambda i, j: (layer, j, 0)),
            _const_spec((1, d)),
        ],
        out_specs=pl.BlockSpec((tm, d), lambda i, j: (i, 0)),
        out_shape=jax.ShapeDtypeStruct((t, d), F32),
        scratch_shapes=[pltpu.VMEM((tm, d), BF16), pltpu.VMEM((tm, d), F32)],
        compiler_params=_params(("parallel", "arbitrary")),
        name="mlp",
    )(x, gin, wup, wdn, gout)


def _ple_kernel(x_ref, p_ref, wpu_ref, pn_ref, wg_ref, y_ref):
    x = x_ref[...]
    e = _rms(_dot(p_ref[...].astype(BF16), wpu_ref[...]), pn_ref[...])
    z = _dot(x.astype(BF16), wg_ref[...])
    y_ref[...] = x + e / (1.0 + jnp.exp(-z))


def _ple(x, p, wpu, pn, wg, layer, tm=512):
    t, d = x.shape
    return pl.pallas_call(
        _ple_kernel,
        grid=(t // tm,),
        in_specs=[
            pl.BlockSpec((tm, d), lambda i: (i, 0)),
            pl.BlockSpec((None, tm, p.shape[-1]), lambda i: (layer, i, 0)),
            _layer_spec(wpu, layer),
            _const_spec((1, d)),
            _layer_spec(wg, layer),
        ],
        out_specs=pl.BlockSpec((tm, d), lambda i: (i, 0)),
        out_shape=jax.ShapeDtypeStruct((t, d), F32),
        compiler_params=_params(("parallel",)),
        name="ple",
    )(x, p, wpu, pn, wg)


def _pad_cols(w, width):
    return jnp.pad(w, ((0, 0), (0, width - w.shape[1])))


def _swap_halves(w):
    half = w.shape[1] // 2
    return jnp.concatenate([w[:, half:], w[:, :half]], axis=1)


def _prep_mla_weights(w_down, w_uq):
    base = Q_LORA + KV_LORA
    w_kr = w_down[:, base:]
    wd = jnp.concatenate([w_down[:, :base], _pad_cols(w_kr, LANES), _pad_cols(_swap_halves(w_kr), LANES)], axis=1)
    wq = w_uq.reshape(Q_LORA, MLA_HEADS, NOPE_DIM + ROPE_DIM)
    nope = wq[:, :, :NOPE_DIM].reshape(Q_LORA, MLA_HEADS * NOPE_DIM)
    rope = wq[:, :, NOPE_DIM:]
    half = ROPE_DIM // 2
    swapped = jnp.concatenate([rope[:, :, half:], rope[:, :, :half]], axis=2)
    rope = rope.reshape(Q_LORA, MLA_HEADS * ROPE_DIM)
    swapped = swapped.reshape(Q_LORA, MLA_HEADS * ROPE_DIM)
    return wd.astype(BF16), jnp.concatenate([nope, rope, swapped], axis=1).astype(BF16)


def _rope_tables(seq):
    half = ROPE_DIM // 2
    inv = 1.0 / (ROPE_THETA ** (jnp.arange(half, dtype=F32) / half))
    ang = jnp.arange(seq).astype(F32)[:, None] * inv[None, :]
    cos = jnp.cos(ang)
    sin = jnp.sin(ang)
    return jnp.tile(cos, (1, LANES // half)), jnp.tile(jnp.concatenate([-sin, sin], axis=1), (1, LANES // ROPE_DIM))


def _trunk(x, p, batch, seq, w):
    cos, sin = _rope_tables(seq)
    for i in range(DEPTH):
        g = w["norm_gains"][i]
        j = i // N_MIXERS
        if i % N_MIXERS == 0:
            q, k, v = _mla_proj(x, g[0:1], w["mla_w_down"][j], w["mla_q_norm"][j:j + 1], w["mla_kv_norm"][j:j + 1],
                                w["mla_w_uq"][j], w["mla_w_ukv"], j, cos, sin, seq)
            o = _mla_attn(q, k, v, batch, seq)
            x = _out_proj_residual(o, w["mla_w_o"], j, g[1:2], x)
        else:
            qt, k, vt = _swa_proj(x, g[0:1], w["swa_w_qkv"], j)
            ot = _swa_attn(qt, k, vt, w["swa_sink"][j], w["bias_table"], batch, seq)
            x = _out_proj_residual(ot, w["swa_w_o"], j, g[1:2], x, feature_major=True)
        x = _mlp(x, g[2:3], w["mlp_w_up"], w["mlp_w_down"], g[3:4], i)
        x = _ple(x, p, w["ple_w_up"], w["ple_norm"][i:i + 1], w["ple_w_gate"], i)
    return x


def _ff_slabs(w_up, tf=1024):
    n, d, f = w_up.shape
    return w_up.reshape(n, d, f // tf, tf).transpose(0, 2, 1, 3)


def _prep_weights(norm_gains, mla_w_down, mla_q_norm, mla_kv_norm, mla_w_uq, mla_w_ukv, mla_w_o, swa_w_qkv,
                  swa_sink, swa_w_o, rel_bias, mlp_w_up, mlp_w_down, ple_w_up, ple_w_gate, ple_norm):
    wd, wq = zip(*[_prep_mla_weights(mla_w_down[j], mla_w_uq[j]) for j in range(mla_w_down.shape[0])])
    return dict(
        norm_gains=norm_gains, mla_w_down=wd, mla_q_norm=mla_q_norm, mla_kv_norm=mla_kv_norm, mla_w_uq=wq,
        mla_w_ukv=mla_w_ukv.astype(BF16), mla_w_o=mla_w_o.astype(BF16), swa_w_qkv=swa_w_qkv.astype(BF16),
        swa_sink=swa_sink, swa_w_o=swa_w_o.astype(BF16), bias_table=_bias_table(rel_bias),
        mlp_w_up=_ff_slabs(mlp_w_up).astype(BF16), mlp_w_down=mlp_w_down.astype(BF16), ple_w_up=ple_w_up.astype(BF16),
        ple_w_gate=ple_w_gate.astype(BF16), ple_norm=ple_norm)


def kernel(x_prompt, x_sample, p_prompt, p_sample, norm_gains, mla_w_down, mla_q_norm, mla_kv_norm, mla_w_uq,
           mla_w_ukv, mla_w_o, swa_w_qkv, swa_sink, swa_w_o, rel_bias, mlp_w_up, mlp_w_down, ple_w_up, ple_w_gate,
           ple_norm):
    w = _prep_weights(norm_gains, mla_w_down, mla_q_norm, mla_kv_norm, mla_w_uq, mla_w_ukv, mla_w_o, swa_w_qkv,
                      swa_sink, swa_w_o, rel_bias, mlp_w_up, mlp_w_down, ple_w_up, ple_w_gate, ple_norm)
    outs = []
    for x, p in ((x_prompt, p_prompt), (x_sample, p_sample)):
        b, s, d = x.shape
        y = _trunk(x.reshape(b * s, d), p.reshape(DEPTH, b * s, p.shape[-1]), b, s, w)
        outs.append(y.reshape(b, s, d))
    return tuple(outs)
```

```python
import functools

import numpy as np
import jax
import jax.numpy as jnp
from jax import lax
from jax.experimental import pallas as pl
from jax.experimental.pallas import tpu as pltpu

D_MODEL = 2048
DEPTH = 4
N_MIXERS = 2
MLA_HEADS = 16
Q_LORA = 512
KV_LORA = 512
NOPE_DIM = 128
ROPE_DIM = 64
V_DIM = 128
ROPE_THETA = 10000.0
SWA_Q_HEADS = 16
SWA_KV_HEADS = 4
SWA_GROUP = SWA_Q_HEADS // SWA_KV_HEADS
SWA_HEAD_DIM = 128
WINDOW = 128
BLOCK = 128
N_BUCKETS = 32
MAX_DISTANCE = 128
D_FF = 4 * D_MODEL
PLE_DIM = 256
EPS = 1e-6
NEG_INF = -1e30

LANES = 128
QK_PAD = 2 * LANES
ONES_COL = NOPE_DIM + ROPE_DIM
K2_COL = ONES_COL + 1
BOUND_MARGIN = 1.02
L_MIN = 2.0 ** -64
LOG2E = float(np.log2(np.e))
F32_ROWS = 8
BF16_ROWS = 16
VMEM_LIMIT = 56 * 1024 * 1024

F32 = jnp.float32
BF16 = jnp.bfloat16


def _rms(x, g):
    return x * lax.rsqrt(jnp.mean(x * x, axis=-1, keepdims=True) + EPS) * g


def _dot(a, b):
    return jnp.dot(a, b, preferred_element_type=F32)


def _const_spec(shape, **kw):
    nd = len(shape)
    return pl.BlockSpec(shape, lambda *_: (0,) * nd, **kw)


def _layer_spec(w, layer, **kw):
    return pl.BlockSpec((None,) + w.shape[1:], lambda *_: (layer, 0, 0), **kw)


def _params(sem):
    return pltpu.CompilerParams(dimension_semantics=sem, vmem_limit_bytes=VMEM_LIMIT)


def _swa_proj_kernel(x_ref, g_ref, w_ref, qt_ref, k_ref, vt_ref, *, q_scale):
    h = _rms(x_ref[...], g_ref[...]).astype(BF16)
    dh = SWA_HEAD_DIM
    q_w = SWA_Q_HEADS * dh
    kv_w = SWA_KV_HEADS * dh
    for c in range(0, q_w, kv_w):
        y = _dot(h, w_ref[:, c:c + kv_w]) * q_scale
        for hq in range(kv_w // dh):
            qt_ref[c + hq * dh:c + (hq + 1) * dh, :] = y[:, hq * dh:(hq + 1) * dh].T.astype(BF16)
    k_ref[...] = _dot(h, w_ref[:, q_w:q_w + kv_w]).astype(BF16)
    v = _dot(h, w_ref[:, q_w + kv_w:])
    for kh in range(SWA_KV_HEADS):
        vt_ref[kh * dh:(kh + 1) * dh, :] = v[:, kh * dh:(kh + 1) * dh].T.astype(BF16)


def _swa_proj(x, g, w, layer, tm=512):
    t, d = x.shape
    q_w = SWA_Q_HEADS * SWA_HEAD_DIM
    kv_w = SWA_KV_HEADS * SWA_HEAD_DIM
    return pl.pallas_call(
        functools.partial(_swa_proj_kernel, q_scale=float(SWA_HEAD_DIM ** -0.5) * LOG2E),
        grid=(t // tm,),
        in_specs=[pl.BlockSpec((tm, d), lambda i: (i, 0)), _const_spec((1, d)), _layer_spec(w, layer)],
        out_specs=[
            pl.BlockSpec((q_w, tm), lambda i: (0, i)),
            pl.BlockSpec((tm, kv_w), lambda i: (i, 0)),
            pl.BlockSpec((kv_w, tm), lambda i: (0, i)),
        ],
        out_shape=[
            jax.ShapeDtypeStruct((q_w, t), BF16),
            jax.ShapeDtypeStruct((t, kv_w), BF16),
            jax.ShapeDtypeStruct((kv_w, t), BF16),
        ],
        compiler_params=_params(("parallel",)),
        name="swa_proj",
    )(x, g, w)


def _mla_proj_kernel(x_ref, g_ref, wd_ref, qn_ref, kvn_ref, wuq_ref, wukv_ref, cos_ref, sin_ref,
                     q_ref, k_ref, v_ref, *, scale):
    h = _rms(x_ref[...], g_ref[...]).astype(BF16)
    lat = _dot(h, wd_ref[...])
    cq = _rms(lat[:, :Q_LORA], qn_ref[...]).astype(BF16)
    ckv = _rms(lat[:, Q_LORA:Q_LORA + KV_LORA], kvn_ref[...]).astype(BF16)
    cos = cos_ref[...]
    sin = sin_ref[...]
    base = Q_LORA + KV_LORA
    kr = lat[:, base:base + LANES] * cos + lat[:, base + LANES:base + 2 * LANES] * sin
    kr2 = jnp.sum(kr * kr, axis=1, keepdims=True)
    lane = lax.broadcasted_iota(jnp.int32, kr.shape, 1)
    hn = MLA_HEADS * NOPE_DIM
    grp = 4
    rw = grp * ROPE_DIM
    hr = MLA_HEADS * ROPE_DIM
    cos_g = jnp.tile(cos, (1, rw // LANES))
    sin_g = jnp.tile(sin, (1, rw // LANES))
    zero_rows = jnp.zeros((QK_PAD - NOPE_DIM - ROPE_DIM, kr.shape[0]), BF16)
    for h0 in range(0, MLA_HEADS, grp):
        lo = h0 * LANES
        w = grp * LANES
        ro = hn + h0 * ROPE_DIM
        qn = _dot(cq, wuq_ref[:, lo:lo + w]) * scale
        qr = _dot(cq, wuq_ref[:, ro:ro + rw])
        qs = _dot(cq, wuq_ref[:, hr + ro:hr + ro + rw])
        qrot = (qr * cos_g + qs * sin_g) * scale
        kv = _dot(ckv, wukv_ref[:, 2 * lo:2 * lo + 2 * w])
        for g in range(grp):
            hd = h0 + g
            c = g * LANES
            q_ref[hd, :NOPE_DIM, :] = qn[:, c:c + LANES].T.astype(BF16)
            pair_t = qrot[:, (g // 2) * LANES:(g // 2 + 1) * LANES].T
            q_ref[hd, NOPE_DIM:ONES_COL, :] = pair_t[(g % 2) * ROPE_DIM:(g % 2 + 1) * ROPE_DIM].astype(BF16)
            q_ref[hd, ONES_COL:, :] = zero_rows
            kn = kv[:, 2 * c:2 * c + LANES]
            k_ref[hd, :, :LANES] = kn.astype(BF16)
            k2 = (jnp.sum(kn * kn, axis=1, keepdims=True) + kr2) * BOUND_MARGIN
            k_hi = jnp.where(lane == K2_COL - LANES, k2, kr)
            k_ref[hd, :, LANES:] = jnp.where(lane == ONES_COL - LANES, 1.0, k_hi).astype(BF16)
            v_ref[hd] = kv[:, 2 * c + LANES:2 * c + 2 * LANES].T.astype(BF16)


def _mla_proj(x, g, wd, qn, kvn, wuq, wukv, layer, cos, sin, seq, tm=512):
    t, d = x.shape
    nblk = seq // tm
    scale = float((NOPE_DIM + ROPE_DIM) ** -0.5 * np.log2(np.e))
    heads = MLA_HEADS
    once = pl.Buffered(1)
    return pl.pallas_call(
        functools.partial(_mla_proj_kernel, scale=scale),
        grid=(t // tm,),
        in_specs=[
            pl.BlockSpec((tm, d), lambda i: (i, 0)),
            _const_spec((1, d)),
            _const_spec(wd.shape, pipeline_mode=once),
            _const_spec((1, Q_LORA)),
            _const_spec((1, KV_LORA)),
            _const_spec(wuq.shape, pipeline_mode=once),
            _layer_spec(wukv, layer, pipeline_mode=once),
            pl.BlockSpec((tm, LANES), lambda i: (i % nblk, 0)),
            pl.BlockSpec((tm, LANES), lambda i: (i % nblk, 0)),
        ],
        out_specs=[
            pl.BlockSpec((heads, QK_PAD, tm), lambda i: (0, 0, i)),
            pl.BlockSpec((heads, tm, QK_PAD), lambda i: (0, i, 0)),
            pl.BlockSpec((heads, V_DIM, tm), lambda i: (0, 0, i)),
        ],
        out_shape=[
            jax.ShapeDtypeStruct((heads, QK_PAD, t), BF16),
            jax.ShapeDtypeStruct((heads, t, QK_PAD), BF16),
            jax.ShapeDtypeStruct((heads, V_DIM, t), BF16),
        ],
        compiler_params=_params(("parallel",)),
        name="mla_proj",
    )(x, g, wd, qn, kvn, wuq, wukv, cos, sin)


def _mla_attn_fixed_shift_kernel(qt_ref, k_ref, vt_ref, o_ref, l_ref, acc_sc, k2_sc, *, tkc, group):
    seq = k_ref.shape[0]
    qt = qt_ref[...]
    tq = qt.shape[1]

    @pl.when(pl.program_id(2) == 0)
    def _():
        colmax = jnp.max(k_ref[:, LANES:].astype(F32), axis=0, keepdims=True)
        lane = lax.broadcasted_iota(jnp.int32, colmax.shape, 1)
        k2max = jnp.max(jnp.where(lane == K2_COL - LANES, colmax, 0.0), axis=1, keepdims=True)
        k2_sc[...] = jnp.broadcast_to(k2max, k2_sc.shape)

    qf = qt.astype(F32)
    q2 = jnp.sum(qf * qf, axis=0, keepdims=True)
    shift = -(jnp.sqrt(q2 * jnp.tile(k2_sc[0:1, :], (1, tq // LANES))) * BOUND_MARGIN)
    row = lax.broadcasted_iota(jnp.int32, qt.shape, 0)
    q_aug = jnp.where(row == ONES_COL, shift.astype(BF16), qt)

    def chunk(c):
        off = pl.multiple_of(c * tkc, tkc)
        p = jnp.exp2(_dot(k_ref[pl.ds(off, tkc), :], q_aug))
        return _dot(vt_ref[:, pl.ds(off, tkc)], p.astype(BF16)), jnp.sum(p, axis=0, keepdims=True)

    def body(j, l_run):
        upd, l_new = chunk(group * j)
        for g in range(1, group):
            u, l = chunk(group * j + g)
            upd = upd + u
            l_new = l_new + l
        acc_sc[...] += upd
        return l_run + l_new

    acc_sc[...] = jnp.zeros(acc_sc.shape, F32)
    l = lax.fori_loop(0, seq // (group * tkc), body, jnp.zeros((1, tq), F32))
    l_ref[...] = l
    o_ref[...] = (acc_sc[...] * (1.0 / l)).T.astype(o_ref.dtype)


def _mla_attn_fixed_shift(qt, k, vt, batch, seq, tq=2048, tkc=1024, max_group=8):
    heads, _, t = qt.shape
    nq = seq // tq
    nchunks = seq // tkc
    group = min(max_group, max(nchunks // 2, 1))
    assert nchunks % group == 0
    return pl.pallas_call(
        functools.partial(_mla_attn_fixed_shift_kernel, tkc=tkc, group=group),
        grid=(batch, heads, nq),
        in_specs=[
            pl.BlockSpec((None, QK_PAD, tq), lambda b, h, i: (h, 0, b * nq + i)),
            pl.BlockSpec((None, seq, QK_PAD), lambda b, h, i: (h, b, 0)),
            pl.BlockSpec((None, V_DIM, seq), lambda b, h, i: (h, 0, b)),
        ],
        out_specs=[
            pl.BlockSpec((tq, V_DIM), lambda b, h, i: (b * nq + i, h)),
            pl.BlockSpec((None, 1, tq), lambda b, h, i: (h, 0, b * nq + i)),
        ],
        out_shape=[jax.ShapeDtypeStruct((t, heads * V_DIM), BF16), jax.ShapeDtypeStruct((heads, 1, t), F32)],
        scratch_shapes=[pltpu.VMEM((V_DIM, tq), F32), pltpu.VMEM((8, LANES), F32)],
        compiler_params=_params(("parallel", "parallel", "arbitrary")),
        name="mla_attn_fixed_shift",
    )(qt, k, vt)


def _mla_attn_online_max_kernel(qt_ref, k_ref, vt_ref, o_ref, sa_sc, sb_sc, acc_sc, *, tkc):
    seq = k_ref.shape[0]
    npairs = seq // (2 * tkc)
    qt = qt_ref[...]
    tq = qt.shape[1]

    def scores(c, s_sc):
        off = pl.multiple_of(c * tkc, tkc)
        s = _dot(k_ref[pl.ds(off, tkc), :], qt)
        s_sc[...] = s
        return jnp.max(s, axis=0, keepdims=True)

    def accumulate(c, s_sc, m_run, l_run, m_chunk):
        off = pl.multiple_of(c * tkc, tkc)
        m_new = jnp.maximum(m_run, m_chunk)
        alpha = jnp.exp2(m_run - m_new)
        p = jnp.exp2(s_sc[...] - m_new)
        l_new = alpha * l_run + jnp.sum(p, axis=0, keepdims=True)
        pv = _dot(vt_ref[:, pl.ds(off, tkc)], p.astype(BF16))
        acc_sc[...] = alpha * acc_sc[...] + pv
        return m_new, l_new

    def pair(j, carry, last):
        m_run, l_run, m_a = carry
        m_b = scores(2 * j + 1, sb_sc)
        m_run, l_run = accumulate(2 * j, sa_sc, m_run, l_run, m_a)
        if not last:
            m_a = scores(2 * j + 2, sa_sc)
        m_run, l_run = accumulate(2 * j + 1, sb_sc, m_run, l_run, m_b)
        return m_run, l_run, m_a

    acc_sc[...] = jnp.zeros(acc_sc.shape, F32)
    init = (jnp.full((1, tq), -jnp.inf, F32), jnp.zeros((1, tq), F32), scores(0, sa_sc))
    carry = lax.fori_loop(0, npairs - 1, functools.partial(pair, last=False), init)
    _, l_run, _ = pair(npairs - 1, carry, last=True)
    o_ref[...] = (acc_sc[...] * (1.0 / l_run)).T.astype(o_ref.dtype)


def _mla_attn_online_max(qt, k, vt, batch, seq, tq=1024, tkc=1024):
    heads, _, t = qt.shape
    nq = seq // tq
    return pl.pallas_call(
        functools.partial(_mla_attn_online_max_kernel, tkc=tkc),
        grid=(batch, heads, nq),
        in_specs=[
            pl.BlockSpec((None, QK_PAD, tq), lambda b, h, i: (h, 0, b * nq + i)),
            pl.BlockSpec((None, seq, QK_PAD), lambda b, h, i: (h, b, 0)),
            pl.BlockSpec((None, V_DIM, seq), lambda b, h, i: (h, 0, b)),
        ],
        out_specs=pl.BlockSpec((tq, V_DIM), lambda b, h, i: (b * nq + i, h)),
        out_shape=jax.ShapeDtypeStruct((t, heads * V_DIM), BF16),
        scratch_shapes=[
            pltpu.VMEM((tkc, tq), F32),
            pltpu.VMEM((tkc, tq), F32),
            pltpu.VMEM((V_DIM, tq), F32),
        ],
        compiler_params=_params(("parallel", "parallel", "arbitrary")),
        name="mla_attn_online_max",
    )(qt, k, vt)


def _mla_attn(qt, k, vt, batch, seq):
    o, l = _mla_attn_fixed_shift(qt, k, vt, batch, seq)
    return lax.cond(jnp.min(l) >= L_MIN, lambda: o, lambda: _mla_attn_online_max(qt, k, vt, batch, seq))


def _t5_bucket(rel):
    nb = N_BUCKETS // 2
    max_exact = nb // 2
    ret = (rel > 0).astype(np.int32) * nb
    n = np.abs(rel)
    large = max_exact + (np.log(np.maximum(n, 1).astype(np.float32) / max_exact)
                         / np.log(MAX_DISTANCE / max_exact) * (nb - max_exact)).astype(np.int32)
    large = np.minimum(large, nb - 1)
    return (ret + np.where(n < max_exact, n, large)).astype(np.int32)


def _bias_table_kernel(rb_ref, bucket_ref, o_ref):
    hd = pl.program_id(0)
    bucket = bucket_ref[...]
    acc = jnp.zeros(bucket.shape, F32)
    for b in range(N_BUCKETS):
        acc = jnp.where(bucket == b, rb_ref[b, hd], acc)
    si = lax.broadcasted_iota(jnp.int32, bucket.shape, 0)
    qi = lax.broadcasted_iota(jnp.int32, bucket.shape, 1)
    o_ref[...] = jnp.where(jnp.abs(si - BLOCK - qi) <= WINDOW, acc * LOG2E, NEG_INF)


def _bias_table(rel_bias):
    si = np.arange(3 * BLOCK)[:, None]
    qi = np.arange(BLOCK)[None, :]
    bucket = jnp.asarray(_t5_bucket(si - BLOCK - qi))
    return pl.pallas_call(
        _bias_table_kernel,
        grid=(SWA_Q_HEADS,),
        in_specs=[pl.BlockSpec(memory_space=pltpu.SMEM), _const_spec((3 * BLOCK, BLOCK))],
        out_specs=pl.BlockSpec((None, 3 * BLOCK, BLOCK), lambda h: (h, 0, 0)),
        out_shape=jax.ShapeDtypeStruct((SWA_Q_HEADS, 3 * BLOCK, BLOCK), F32),
        name="t5_bias_table",
    )(rel_bias, bucket)


def _swa_attn_kernel(sink_ref, qt_ref, kp_ref, kc_ref, kn_ref, vp_ref, vc_ref, vn_ref, bias_ref, ot_ref,
                     sa_sc, sb_sc, *, nqb, nblocks):
    i = pl.program_id(1)
    dh = SWA_HEAD_DIM
    kband = jnp.concatenate([kp_ref[...], kc_ref[...], kn_ref[...]], axis=0)
    vband = jnp.concatenate([vp_ref[...], vc_ref[...], vn_ref[...]], axis=1)
    units = [(j, kh) for j in range(nqb) for kh in range(SWA_KV_HEADS)]
    bufs = (sa_sc, sb_sc)

    def scores(unit, s_sc):
        j, kh = unit
        r0 = j * BLOCK
        kb = kband[r0:r0 + 3 * BLOCK, kh * dh:(kh + 1) * dh]
        heads = range(kh * SWA_GROUP, (kh + 1) * SWA_GROUP)
        qt = jnp.concatenate([qt_ref[hq * dh:(hq + 1) * dh, r0:r0 + BLOCK] for hq in heads], axis=1)
        s_sc[...] = _dot(kb, qt)

    def run(at_sequence_edge):
        si = lax.broadcasted_iota(jnp.int32, (3 * BLOCK, 1), 0)
        scores(units[0], bufs[0])
        for u, (j, kh) in enumerate(units):
            if u + 1 < len(units):
                scores(units[u + 1], bufs[(u + 1) % 2])
            r0 = j * BLOCK
            vbt = vband[kh * dh:(kh + 1) * dh, r0:r0 + 3 * BLOCK]
            heads = range(kh * SWA_GROUP, (kh + 1) * SWA_GROUP)
            bias = jnp.concatenate([bias_ref[hq] for hq in heads], axis=1)
            s = bufs[u % 2][...] + bias
            if at_sequence_edge:
                blk = i * nqb + j
                lo_ok = jnp.logical_or(si >= BLOCK, blk > 0)
                hi_ok = jnp.logical_or(si < 2 * BLOCK, blk < nblocks - 1)
                s = s + jnp.where(lo_ok & hi_ok, 0.0, NEG_INF)
            sk = jnp.concatenate([jnp.full((1, BLOCK), sink_ref[hq] * LOG2E, F32) for hq in heads], axis=1)
            m = jnp.maximum(jnp.max(s, axis=0, keepdims=True), sk)
            e = jnp.exp2(s - m)
            denom = jnp.sum(e, axis=0, keepdims=True) + jnp.exp2(sk - m)
            ot = _dot(vbt, e.astype(BF16)) * (1.0 / denom)
            for g, hq in enumerate(heads):
                ot_ref[hq * dh:(hq + 1) * dh, r0:r0 + BLOCK] = ot[:, g * BLOCK:(g + 1) * BLOCK].astype(ot_ref.dtype)

    edge = jnp.logical_or(i == 0, i == pl.num_programs(1) - 1)
    pl.when(edge)(functools.partial(run, True))
    pl.when(jnp.logical_not(edge))(functools.partial(run, False))


def _swa_attn(qt, k, vt, sink, bias, batch, seq, tb=512):
    hq_w, t = qt.shape
    kv_w = k.shape[1]
    nqb = tb // BLOCK
    nsteps = seq // tb
    nblocks = seq // BLOCK

    def prev_blk(b, i):
        return b * nblocks + jnp.maximum(i * nqb - 1, 0)

    def next_blk(b, i):
        return b * nblocks + jnp.minimum((i + 1) * nqb, nblocks - 1)

    return pl.pallas_call(
        functools.partial(_swa_attn_kernel, nqb=nqb, nblocks=nblocks),
        grid=(batch, nsteps),
        in_specs=[
            pl.BlockSpec(memory_space=pltpu.SMEM),
            pl.BlockSpec((hq_w, tb), lambda b, i: (0, b * nsteps + i)),
            pl.BlockSpec((BLOCK, kv_w), lambda b, i: (prev_blk(b, i), 0)),
            pl.BlockSpec((tb, kv_w), lambda b, i: (b * nsteps + i, 0)),
            pl.BlockSpec((BLOCK, kv_w), lambda b, i: (next_blk(b, i), 0)),
            pl.BlockSpec((kv_w, BLOCK), lambda b, i: (0, prev_blk(b, i))),
            pl.BlockSpec((kv_w, tb), lambda b, i: (0, b * nsteps + i)),
            pl.BlockSpec((kv_w, BLOCK), lambda b, i: (0, next_blk(b, i))),
            _const_spec(bias.shape),
        ],
        out_specs=pl.BlockSpec((hq_w, tb), lambda b, i: (0, b * nsteps + i)),
        out_shape=jax.ShapeDtypeStruct((hq_w, t), BF16),
        scratch_shapes=[pltpu.VMEM((3 * BLOCK, SWA_GROUP * BLOCK), F32)] * 2,
        compiler_params=_params(("parallel", "parallel")),
        name="swa_attn",
    )(sink, qt, k, k, k, vt, vt, vt, bias)


def _out_proj_kernel(o_ref, w_ref, g_ref, x_ref, y_ref, *, feature_major):
    if feature_major:
        y = lax.dot_general(o_ref[...], w_ref[...], (((0,), (0,)), ((), ())), preferred_element_type=F32)
    else:
        y = _dot(o_ref[...], w_ref[...])
    y_ref[...] = x_ref[...] + _rms(y, g_ref[...])


def _out_proj_residual(o, w, layer, g, x, feature_major=False, tm=512):
    t, d = x.shape
    if feature_major:
        o_spec = pl.BlockSpec((o.shape[0], tm), lambda i: (0, i))
    else:
        o_spec = pl.BlockSpec((tm, o.shape[1]), lambda i: (i, 0))
    return pl.pallas_call(
        functools.partial(_out_proj_kernel, feature_major=feature_major),
        grid=(t // tm,),
        in_specs=[
            o_spec,
            _layer_spec(w, layer),
            _const_spec((1, d)),
            pl.BlockSpec((tm, d), lambda i: (i, 0)),
        ],
        out_specs=pl.BlockSpec((tm, d), lambda i: (i, 0)),
        out_shape=jax.ShapeDtypeStruct((t, d), F32),
        compiler_params=_params(("parallel",)),
        name="out_proj_residual",
    )(o, w, g, x)


def _mlp_kernel(x_ref, gin_ref, wup_ref, wdn_ref, gout_ref, y_ref, h_sc, acc_sc):
    j = pl.program_id(1)
    tm = x_ref.shape[0]

    @pl.when(j == 0)
    def _():
        g = gin_ref[...]
        for r in range(0, tm, BF16_ROWS):
            h_sc[r:r + BF16_ROWS, :] = _rms(x_ref[r:r + BF16_ROWS, :], g).astype(BF16)
        acc_sc[...] = jnp.zeros(acc_sc.shape, F32)

    u = jnp.maximum(_dot(h_sc[...], wup_ref[...]), 0.0)
    acc_sc[...] += _dot((u * u).astype(BF16), wdn_ref[...])

    @pl.when(j == pl.num_programs(1) - 1)
    def _():
        g = gout_ref[...]
        for r in range(0, tm, F32_ROWS):
            y_ref[r:r + F32_ROWS, :] = x_ref[r:r + F32_ROWS, :] + _rms(acc_sc[r:r + F32_ROWS, :], g)


def _mlp(x, gin, wup, wdn, gout, layer, tm=512, tf=1024):
    t, d = x.shape
    f = wup.shape[-1]
    return pl.pallas_call(
        _mlp_kernel,
        grid=(t // tm, f // tf),
        in_specs=[
            pl.BlockSpec((tm, d), lambda i, j: (i, 0)),
            _const_spec((1, d)),
            pl.BlockSpec((None, d, tf), lambda i, j: (layer, 0, j)),
            pl.BlockSpec((None, tf, d), lambda i, j: (layer, j, 0)),
            _const_spec((1, d)),
        ],
        out_specs=pl.BlockSpec((tm, d), lambda i, j: (i, 0)),
        out_shape=jax.ShapeDtypeStruct((t, d), F32),
        scratch_shapes=[pltpu.VMEM((tm, d), BF16), pltpu.VMEM((tm, d), F32)],
        compiler_params=_params(("parallel", "arbitrary")),
        name="mlp",
    )(x, gin, wup, wdn, gout)


def _ple_kernel(x_ref, p_ref, wpu_ref, pn_ref, wg_ref, y_ref):
    x = x_ref[...]
    e = _rms(_dot(p_ref[...].astype(BF16), wpu_ref[...]), pn_ref[...])
    z = _dot(x.astype(BF16), wg_ref[...])
    y_ref[...] = x + e / (1.0 + jnp.exp(-z))


def _ple(x, p, wpu, pn, wg, layer, tm=512):
    t, d = x.shape
    return pl.pallas_call(
        _ple_kernel,
        grid=(t // tm,),
        in_specs=[
            pl.BlockSpec((tm, d), lambda i: (i, 0)),
            pl.BlockSpec((None, tm, p.shape[-1]), lambda i: (layer, i, 0)),
            _layer_spec(wpu, layer),
            _const_spec((1, d)),
            _layer_spec(wg, layer),
        ],
        out_specs=pl.BlockSpec((tm, d), lambda i: (i, 0)),
        out_shape=jax.ShapeDtypeStruct((t, d), F32),
        compiler_params=_params(("parallel",)),
        name="ple",
    )(x, p, wpu, pn, wg)


def _pad_cols(w, width):
    return jnp.pad(w, ((0, 0), (0, width - w.shape[1])))


def _swap_halves(w):
    half = w.shape[1] // 2
    return jnp.concatenate([w[:, half:], w[:, :half]], axis=1)


def _prep_mla_weights(w_down, w_uq):
    base = Q_LORA + KV_LORA
    w_kr = w_down[:, base:]
    wd = jnp.concatenate([w_down[:, :base], _pad_cols(w_kr, LANES), _pad_cols(_swap_halves(w_kr), LANES)], axis=1)
    wq = w_uq.reshape(Q_LORA, MLA_HEADS, NOPE_DIM + ROPE_DIM)
    nope = wq[:, :, :NOPE_DIM].reshape(Q_LORA, MLA_HEADS * NOPE_DIM)
    rope = wq[:, :, NOPE_DIM:]
    half = ROPE_DIM // 2
    swapped = jnp.concatenate([rope[:, :, half:], rope[:, :, :half]], axis=2)
    rope = rope.reshape(Q_LORA, MLA_HEADS * ROPE_DIM)
    swapped = swapped.reshape(Q_LORA, MLA_HEADS * ROPE_DIM)
    return wd.astype(BF16), jnp.concatenate([nope, rope, swapped], axis=1).astype(BF16)


def _rope_tables(seq):
    half = ROPE_DIM // 2
    inv = 1.0 / (ROPE_THETA ** (jnp.arange(half, dtype=F32) / half))
    ang = jnp.arange(seq).astype(F32)[:, None] * inv[None, :]
    cos = jnp.cos(ang)
    sin = jnp.sin(ang)
    return jnp.tile(cos, (1, LANES // half)), jnp.tile(jnp.concatenate([-sin, sin], axis=1), (1, LANES // ROPE_DIM))


def _trunk(x, p, batch, seq, w):
    cos, sin = _rope_tables(seq)
    for i in range(DEPTH):
        g = w["norm_gains"][i]
        j = i // N_MIXERS
        if i % N_MIXERS == 0:
            q, k, v = _mla_proj(x, g[0:1], w["mla_w_down"][j], w["mla_q_norm"][j:j + 1], w["mla_kv_norm"][j:j + 1],
                                w["mla_w_uq"][j], w["mla_w_ukv"], j, cos, sin, seq)
            o = _mla_attn(q, k, v, batch, seq)
            x = _out_proj_residual(o, w["mla_w_o"], j, g[1:2], x)
        else:
            qt, k, vt = _swa_proj(x, g[0:1], w["swa_w_qkv"], j)
            ot = _swa_attn(qt, k, vt, w["swa_sink"][j], w["bias_table"], batch, seq)
            x = _out_proj_residual(ot, w["swa_w_o"], j, g[1:2], x, feature_major=True)
        x = _mlp(x, g[2:3], w["mlp_w_up"], w["mlp_w_down"], g[3:4], i)
        x = _ple(x, p, w["ple_w_up"], w["ple_norm"][i:i + 1], w["ple_w_gate"], i)
    return x


def _prep_weights(norm_gains, mla_w_down, mla_q_norm, mla_kv_norm, mla_w_uq, mla_w_ukv, mla_w_o, swa_w_qkv,
                  swa_sink, swa_w_o, rel_bias, mlp_w_up, mlp_w_down, ple_w_up, ple_w_gate, ple_norm):
    wd, wq = zip(*[_prep_mla_weights(mla_w_down[j], mla_w_uq[j]) for j in range(mla_w_down.shape[0])])
    return dict(
        norm_gains=norm_gains, mla_w_down=wd, mla_q_norm=mla_q_norm, mla_kv_norm=mla_kv_norm, mla_w_uq=wq,
        mla_w_ukv=mla_w_ukv.astype(BF16), mla_w_o=mla_w_o.astype(BF16), swa_w_qkv=swa_w_qkv.astype(BF16),
        swa_sink=swa_sink, swa_w_o=swa_w_o.astype(BF16), bias_table=_bias_table(rel_bias),
        mlp_w_up=mlp_w_up.astype(BF16), mlp_w_down=mlp_w_down.astype(BF16), ple_w_up=ple_w_up.astype(BF16),
        ple_w_gate=ple_w_gate.astype(BF16), ple_norm=ple_norm)


def kernel(x_prompt, x_sample, p_prompt, p_sample, norm_gains, mla_w_down, mla_q_norm, mla_kv_norm, mla_w_uq,
           mla_w_ukv, mla_w_o, swa_w_qkv, swa_sink, swa_w_o, rel_bias, mlp_w_up, mlp_w_down, ple_w_up, ple_w_gate,
           ple_norm):
    w = _prep_weights(norm_gains, mla_w_down, mla_q_norm, mla_kv_norm, mla_w_uq, mla_w_ukv, mla_w_o, swa_w_qkv,
                      swa_sink, swa_w_o, rel_bias, mlp_w_up, mlp_w_down, ple_w_up, ple_w_gate, ple_norm)
    outs = []
    for x, p in ((x_prompt, p_prompt), (x_sample, p_sample)):
        b, s, d = x.shape
        y = _trunk(x.reshape(b * s, d), p.reshape(DEPTH, b * s, p.shape[-1]), b, s, w)
        outs.append(y.reshape(b, s, d))
    return tuple(outs)
```

```python
import functools

import numpy as np
import jax
import jax.numpy as jnp
from jax import lax
from jax.experimental import pallas as pl
from jax.experimental.pallas import tpu as pltpu

D_MODEL = 2048
DEPTH = 4
N_MIXERS = 2
MLA_HEADS = 16
Q_LORA = 512
KV_LORA = 512
NOPE_DIM = 128
ROPE_DIM = 64
V_DIM = 128
ROPE_THETA = 10000.0
SWA_Q_HEADS = 16
SWA_KV_HEADS = 4
SWA_GROUP = SWA_Q_HEADS // SWA_KV_HEADS
SWA_HEAD_DIM = 128
WINDOW = 128
BLOCK = 128
N_BUCKETS = 32
MAX_DISTANCE = 128
D_FF = 4 * D_MODEL
PLE_DIM = 256
EPS = 1e-6
NEG_INF = -1e30

LANES = 128
QK_PAD = 2 * LANES
ONES_COL = NOPE_DIM + ROPE_DIM
K2_COL = ONES_COL + 1
BOUND_MARGIN = 1.02
L_MIN = 2.0 ** -64
LOG2E = float(np.log2(np.e))
F32_ROWS = 8
BF16_ROWS = 16
VMEM_LIMIT = 56 * 1024 * 1024

F32 = jnp.float32
BF16 = jnp.bfloat16


def _rms(x, g):
    return x * lax.rsqrt(jnp.mean(x * x, axis=-1, keepdims=True) + EPS) * g


def _dot(a, b):
    return jnp.dot(a, b, preferred_element_type=F32)


def _const_spec(shape, **kw):
    nd = len(shape)
    return pl.BlockSpec(shape, lambda *_: (0,) * nd, **kw)


def _layer_spec(w, layer, **kw):
    return pl.BlockSpec((None,) + w.shape[1:], lambda *_: (layer, 0, 0), **kw)


def _params(sem):
    return pltpu.CompilerParams(dimension_semantics=sem, vmem_limit_bytes=VMEM_LIMIT)


def _swa_proj_kernel(x_ref, g_ref, w_ref, qt_ref, k_ref, vt_ref, *, q_scale):
    h = _rms(x_ref[...], g_ref[...]).astype(BF16)
    dh = SWA_HEAD_DIM
    q_w = SWA_Q_HEADS * dh
    kv_w = SWA_KV_HEADS * dh
    for c in range(0, q_w, kv_w):
        y = _dot(h, w_ref[:, c:c + kv_w]) * q_scale
        for hq in range(kv_w // dh):
            qt_ref[c + hq * dh:c + (hq + 1) * dh, :] = y[:, hq * dh:(hq + 1) * dh].T.astype(BF16)
    k_ref[...] = _dot(h, w_ref[:, q_w:q_w + kv_w]).astype(BF16)
    v = _dot(h, w_ref[:, q_w + kv_w:])
    for kh in range(SWA_KV_HEADS):
        vt_ref[kh * dh:(kh + 1) * dh, :] = v[:, kh * dh:(kh + 1) * dh].T.astype(BF16)


def _swa_proj(x, g, w, layer, tm=512):
    t, d = x.shape
    q_w = SWA_Q_HEADS * SWA_HEAD_DIM
    kv_w = SWA_KV_HEADS * SWA_HEAD_DIM
    return pl.pallas_call(
        functools.partial(_swa_proj_kernel, q_scale=float(SWA_HEAD_DIM ** -0.5) * LOG2E),
        grid=(t // tm,),
        in_specs=[pl.BlockSpec((tm, d), lambda i: (i, 0)), _const_spec((1, d)), _layer_spec(w, layer)],
        out_specs=[
            pl.BlockSpec((q_w, tm), lambda i: (0, i)),
            pl.BlockSpec((tm, kv_w), lambda i: (i, 0)),
            pl.BlockSpec((kv_w, tm), lambda i: (0, i)),
        ],
        out_shape=[
            jax.ShapeDtypeStruct((q_w, t), BF16),
            jax.ShapeDtypeStruct((t, kv_w), BF16),
            jax.ShapeDtypeStruct((kv_w, t), BF16),
        ],
        compiler_params=_params(("parallel",)),
        name="swa_proj",
    )(x, g, w)


def _mla_proj_kernel(x_ref, g_ref, wd_ref, qn_ref, kvn_ref, wuq_ref, wukv_ref, cos_ref, sin_ref,
                     q_ref, k_ref, v_ref, *, scale):
    h = _rms(x_ref[...], g_ref[...]).astype(BF16)
    lat = _dot(h, wd_ref[...])
    cq = _rms(lat[:, :Q_LORA], qn_ref[...]).astype(BF16)
    ckv = _rms(lat[:, Q_LORA:Q_LORA + KV_LORA], kvn_ref[...]).astype(BF16)
    cos = cos_ref[...]
    sin = sin_ref[...]
    base = Q_LORA + KV_LORA
    kr = lat[:, base:base + LANES] * cos + lat[:, base + LANES:base + 2 * LANES] * sin
    kr2 = jnp.sum(kr * kr, axis=1, keepdims=True)
    lane = lax.broadcasted_iota(jnp.int32, kr.shape, 1)
    hn = MLA_HEADS * NOPE_DIM
    grp = 4
    rw = grp * ROPE_DIM
    hr = MLA_HEADS * ROPE_DIM
    cos_g = jnp.tile(cos, (1, rw // LANES))
    sin_g = jnp.tile(sin, (1, rw // LANES))
    zero_rows = jnp.zeros((QK_PAD - NOPE_DIM - ROPE_DIM, kr.shape[0]), BF16)
    for h0 in range(0, MLA_HEADS, grp):
        lo = h0 * LANES
        w = grp * LANES
        ro = hn + h0 * ROPE_DIM
        qn = _dot(cq, wuq_ref[:, lo:lo + w]) * scale
        qr = _dot(cq, wuq_ref[:, ro:ro + rw])
        qs = _dot(cq, wuq_ref[:, hr + ro:hr + ro + rw])
        qrot = (qr * cos_g + qs * sin_g) * scale
        kv = _dot(ckv, wukv_ref[:, 2 * lo:2 * lo + 2 * w])
        for g in range(grp):
            hd = h0 + g
            c = g * LANES
            q_ref[hd, :NOPE_DIM, :] = qn[:, c:c + LANES].T.astype(BF16)
            pair_t = qrot[:, (g // 2) * LANES:(g // 2 + 1) * LANES].T
            q_ref[hd, NOPE_DIM:ONES_COL, :] = pair_t[(g % 2) * ROPE_DIM:(g % 2 + 1) * ROPE_DIM].astype(BF16)
            q_ref[hd, ONES_COL:, :] = zero_rows
            kn = kv[:, 2 * c:2 * c + LANES]
            k_ref[hd, :, :LANES] = kn.astype(BF16)
            k2 = (jnp.sum(kn * kn, axis=1, keepdims=True) + kr2) * BOUND_MARGIN
            k_hi = jnp.where(lane == K2_COL - LANES, k2, kr)
            k_ref[hd, :, LANES:] = jnp.where(lane == ONES_COL - LANES, 1.0, k_hi).astype(BF16)
            v_ref[hd] = kv[:, 2 * c + LANES:2 * c + 2 * LANES].T.astype(BF16)


def _mla_proj(x, g, wd, qn, kvn, wuq, wukv, layer, cos, sin, seq, tm=512):
    t, d = x.shape
    nblk = seq // tm
    scale = float((NOPE_DIM + ROPE_DIM) ** -0.5 * np.log2(np.e))
    heads = MLA_HEADS
    once = pl.Buffered(1)
    return pl.pallas_call(
        functools.partial(_mla_proj_kernel, scale=scale),
        grid=(t // tm,),
        in_specs=[
            pl.BlockSpec((tm, d), lambda i: (i, 0)),
            _const_spec((1, d)),
            _const_spec(wd.shape, pipeline_mode=once),
            _const_spec((1, Q_LORA)),
            _const_spec((1, KV_LORA)),
            _const_spec(wuq.shape, pipeline_mode=once),
            _layer_spec(wukv, layer, pipeline_mode=once),
            pl.BlockSpec((tm, LANES), lambda i: (i % nblk, 0)),
            pl.BlockSpec((tm, LANES), lambda i: (i % nblk, 0)),
        ],
        out_specs=[
            pl.BlockSpec((heads, QK_PAD, tm), lambda i: (0, 0, i)),
            pl.BlockSpec((heads, tm, QK_PAD), lambda i: (0, i, 0)),
            pl.BlockSpec((heads, V_DIM, tm), lambda i: (0, 0, i)),
        ],
        out_shape=[
            jax.ShapeDtypeStruct((heads, QK_PAD, t), BF16),
            jax.ShapeDtypeStruct((heads, t, QK_PAD), BF16),
            jax.ShapeDtypeStruct((heads, V_DIM, t), BF16),
        ],
        compiler_params=_params(("parallel",)),
        name="mla_proj",
    )(x, g, wd, qn, kvn, wuq, wukv, cos, sin)


def _mla_attn_fixed_shift_kernel(qt_ref, k_ref, vt_ref, o_ref, l_ref, acc_sc, k2_sc, *, tkc, group):
    seq = k_ref.shape[0]
    qt = qt_ref[...]
    tq = qt.shape[1]

    @pl.when(pl.program_id(2) == 0)
    def _():
        colmax = jnp.max(k_ref[:, LANES:].astype(F32), axis=0, keepdims=True)
        lane = lax.broadcasted_iota(jnp.int32, colmax.shape, 1)
        k2max = jnp.max(jnp.where(lane == K2_COL - LANES, colmax, 0.0), axis=1, keepdims=True)
        k2_sc[...] = jnp.broadcast_to(k2max, k2_sc.shape)

    qf = qt.astype(F32)
    q2 = jnp.sum(qf * qf, axis=0, keepdims=True)
    shift = -(jnp.sqrt(q2 * jnp.tile(k2_sc[0:1, :], (1, tq // LANES))) * BOUND_MARGIN)
    row = lax.broadcasted_iota(jnp.int32, qt.shape, 0)
    q_aug = jnp.where(row == ONES_COL, shift.astype(BF16), qt)

    def chunk(c):
        off = pl.multiple_of(c * tkc, tkc)
        p = jnp.exp2(_dot(k_ref[pl.ds(off, tkc), :], q_aug))
        return _dot(vt_ref[:, pl.ds(off, tkc)], p.astype(BF16)), jnp.sum(p, axis=0, keepdims=True)

    def body(j, l_run):
        upd, l_new = chunk(group * j)
        for g in range(1, group):
            u, l = chunk(group * j + g)
            upd = upd + u
            l_new = l_new + l
        acc_sc[...] += upd
        return l_run + l_new

    acc_sc[...] = jnp.zeros(acc_sc.shape, F32)
    l = lax.fori_loop(0, seq // (group * tkc), body, jnp.zeros((1, tq), F32))
    l_ref[...] = l
    o_ref[...] = (acc_sc[...] * (1.0 / l)).T.astype(o_ref.dtype)


def _mla_attn_fixed_shift(qt, k, vt, batch, seq, tq=2048, tkc=1024, max_group=8):
    heads, _, t = qt.shape
    nq = seq // tq
    nchunks = seq // tkc
    group = min(max_group, max(nchunks // 2, 1))
    assert nchunks % group == 0
    return pl.pallas_call(
        functools.partial(_mla_attn_fixed_shift_kernel, tkc=tkc, group=group),
        grid=(batch, heads, nq),
        in_specs=[
            pl.BlockSpec((None, QK_PAD, tq), lambda b, h, i: (h, 0, b * nq + i)),
            pl.BlockSpec((None, seq, QK_PAD), lambda b, h, i: (h, b, 0)),
            pl.BlockSpec((None, V_DIM, seq), lambda b, h, i: (h, 0, b)),
        ],
        out_specs=[
            pl.BlockSpec((tq, V_DIM), lambda b, h, i: (b * nq + i, h)),
            pl.BlockSpec((None, 1, tq), lambda b, h, i: (h, 0, b * nq + i)),
        ],
        out_shape=[jax.ShapeDtypeStruct((t, heads * V_DIM), BF16), jax.ShapeDtypeStruct((heads, 1, t), F32)],
        scratch_shapes=[pltpu.VMEM((V_DIM, tq), F32), pltpu.VMEM((8, LANES), F32)],
        compiler_params=_params(("parallel", "parallel", "arbitrary")),
        name="mla_attn_fixed_shift",
    )(qt, k, vt)


def _mla_attn_online_max_kernel(qt_ref, k_ref, vt_ref, o_ref, sa_sc, sb_sc, acc_sc, *, tkc):
    seq = k_ref.shape[0]
    npairs = seq // (2 * tkc)
    qt = qt_ref[...]
    tq = qt.shape[1]

    def scores(c, s_sc):
        off = pl.multiple_of(c * tkc, tkc)
        s = _dot(k_ref[pl.ds(off, tkc), :], qt)
        s_sc[...] = s
        return jnp.max(s, axis=0, keepdims=True)

    def accumulate(c, s_sc, m_run, l_run, m_chunk):
        off = pl.multiple_of(c * tkc, tkc)
        m_new = jnp.maximum(m_run, m_chunk)
        alpha = jnp.exp2(m_run - m_new)
        p = jnp.exp2(s_sc[...] - m_new)
        l_new = alpha * l_run + jnp.sum(p, axis=0, keepdims=True)
        pv = _dot(vt_ref[:, pl.ds(off, tkc)], p.astype(BF16))
        acc_sc[...] = alpha * acc_sc[...] + pv
        return m_new, l_new

    def pair(j, carry, last):
        m_run, l_run, m_a = carry
        m_b = scores(2 * j + 1, sb_sc)
        m_run, l_run = accumulate(2 * j, sa_sc, m_run, l_run, m_a)
        if not last:
            m_a = scores(2 * j + 2, sa_sc)
        m_run, l_run = accumulate(2 * j + 1, sb_sc, m_run, l_run, m_b)
        return m_run, l_run, m_a

    acc_sc[...] = jnp.zeros(acc_sc.shape, F32)
    init = (jnp.full((1, tq), -jnp.inf, F32), jnp.zeros((1, tq), F32), scores(0, sa_sc))
    carry = lax.fori_loop(0, npairs - 1, functools.partial(pair, last=False), init)
    _, l_run, _ = pair(npairs - 1, carry, last=True)
    o_ref[...] = (acc_sc[...] * (1.0 / l_run)).T.astype(o_ref.dtype)


def _mla_attn_online_max(qt, k, vt, batch, seq, tq=1024, tkc=1024):
    heads, _, t = qt.shape
    nq = seq // tq
    return pl.pallas_call(
        functools.partial(_mla_attn_online_max_kernel, tkc=tkc),
        grid=(batch, heads, nq),
        in_specs=[
            pl.BlockSpec((None, QK_PAD, tq), lambda b, h, i: (h, 0, b * nq + i)),
            pl.BlockSpec((None, seq, QK_PAD), lambda b, h, i: (h, b, 0)),
            pl.BlockSpec((None, V_DIM, seq), lambda b, h, i: (h, 0, b)),
        ],
        out_specs=pl.BlockSpec((tq, V_DIM), lambda b, h, i: (b * nq + i, h)),
        out_shape=jax.ShapeDtypeStruct((t, heads * V_DIM), BF16),
        scratch_shapes=[
            pltpu.VMEM((tkc, tq), F32),
            pltpu.VMEM((tkc, tq), F32),
            pltpu.VMEM((V_DIM, tq), F32),
        ],
        compiler_params=_params(("parallel", "parallel", "arbitrary")),
        name="mla_attn_online_max",
    )(qt, k, vt)


def _mla_attn(qt, k, vt, batch, seq):
    o, l = _mla_attn_fixed_shift(qt, k, vt, batch, seq)
    return lax.cond(jnp.min(l) >= L_MIN, lambda: o, lambda: _mla_attn_online_max(qt, k, vt, batch, seq))


def _t5_bucket(rel):
    nb = N_BUCKETS // 2
    max_exact = nb // 2
    ret = (rel > 0).astype(np.int32) * nb
    n = np.abs(rel)
    large = max_exact + (np.log(np.maximum(n, 1).astype(np.float32) / max_exact)
                         / np.log(MAX_DISTANCE / max_exact) * (nb - max_exact)).astype(np.int32)
    large = np.minimum(large, nb - 1)
    return (ret + np.where(n < max_exact, n, large)).astype(np.int32)


def _bias_table_kernel(rb_ref, bucket_ref, o_ref):
    hd = pl.program_id(0)
    bucket = bucket_ref[...]
    acc = jnp.zeros(bucket.shape, F32)
    for b in range(N_BUCKETS):
        acc = jnp.where(bucket == b, rb_ref[b, hd], acc)
    si = lax.broadcasted_iota(jnp.int32, bucket.shape, 0)
    qi = lax.broadcasted_iota(jnp.int32, bucket.shape, 1)
    o_ref[...] = jnp.where(jnp.abs(si - BLOCK - qi) <= WINDOW, acc * LOG2E, NEG_INF)


def _bias_table(rel_bias):
    si = np.arange(3 * BLOCK)[:, None]
    qi = np.arange(BLOCK)[None, :]
    bucket = jnp.asarray(_t5_bucket(si - BLOCK - qi))
    return pl.pallas_call(
        _bias_table_kernel,
        grid=(SWA_Q_HEADS,),
        in_specs=[pl.BlockSpec(memory_space=pltpu.SMEM), _const_spec((3 * BLOCK, BLOCK))],
        out_specs=pl.BlockSpec((None, 3 * BLOCK, BLOCK), lambda h: (h, 0, 0)),
        out_shape=jax.ShapeDtypeStruct((SWA_Q_HEADS, 3 * BLOCK, BLOCK), F32),
        name="t5_bias_table",
    )(rel_bias, bucket)


def _swa_attn_kernel(sink_ref, qt_ref, kp_ref, kc_ref, kn_ref, vp_ref, vc_ref, vn_ref, bias_ref, ot_ref,
                     sa_sc, sb_sc, *, nqb, nblocks):
    i = pl.program_id(1)
    dh = SWA_HEAD_DIM
    kband = jnp.concatenate([kp_ref[...], kc_ref[...], kn_ref[...]], axis=0)
    vband = jnp.concatenate([vp_ref[...], vc_ref[...], vn_ref[...]], axis=1)
    units = [(j, kh) for j in range(nqb) for kh in range(SWA_KV_HEADS)]
    bufs = (sa_sc, sb_sc)

    def scores(unit, s_sc):
        j, kh = unit
        r0 = j * BLOCK
        kb = kband[r0:r0 + 3 * BLOCK, kh * dh:(kh + 1) * dh]
        heads = range(kh * SWA_GROUP, (kh + 1) * SWA_GROUP)
        qt = jnp.concatenate([qt_ref[hq * dh:(hq + 1) * dh, r0:r0 + BLOCK] for hq in heads], axis=1)
        s_sc[...] = _dot(kb, qt)

    def run(at_sequence_edge):
        si = lax.broadcasted_iota(jnp.int32, (3 * BLOCK, 1), 0)
        scores(units[0], bufs[0])
        for u, (j, kh) in enumerate(units):
            if u + 1 < len(units):
                scores(units[u + 1], bufs[(u + 1) % 2])
            r0 = j * BLOCK
            vbt = vband[kh * dh:(kh + 1) * dh, r0:r0 + 3 * BLOCK]
            heads = range(kh * SWA_GROUP, (kh + 1) * SWA_GROUP)
            bias = jnp.concatenate([bias_ref[hq] for hq in heads], axis=1)
            s = bufs[u % 2][...] + bias
            if at_sequence_edge:
                blk = i * nqb + j
                lo_ok = jnp.logical_or(si >= BLOCK, blk > 0)
                hi_ok = jnp.logical_or(si < 2 * BLOCK, blk < nblocks - 1)
                s = s + jnp.where(lo_ok & hi_ok, 0.0, NEG_INF)
            sk = jnp.concatenate([jnp.full((1, BLOCK), sink_ref[hq] * LOG2E, F32) for hq in heads], axis=1)
            m = jnp.maximum(jnp.max(s, axis=0, keepdims=True), sk)
            e = jnp.exp2(s - m)
            denom = jnp.sum(e, axis=0, keepdims=True) + jnp.exp2(sk - m)
            ot = _dot(vbt, e.astype(BF16)) * (1.0 / denom)
            for g, hq in enumerate(heads):
                ot_ref[hq * dh:(hq + 1) * dh, r0:r0 + BLOCK] = ot[:, g * BLOCK:(g + 1) * BLOCK].astype(ot_ref.dtype)

    edge = jnp.logical_or(i == 0, i == pl.num_programs(1) - 1)
    pl.when(edge)(functools.partial(run, True))
    pl.when(jnp.logical_not(edge))(functools.partial(run, False))


def _swa_attn(qt, k, vt, sink, bias, batch, seq, tb=512):
    hq_w, t = qt.shape
    kv_w = k.shape[1]
    nqb = tb // BLOCK
    nsteps = seq // tb
    nblocks = seq // BLOCK

    def prev_blk(b, i):
        return b * nblocks + jnp.maximum(i * nqb - 1, 0)

    def next_blk(b, i):
        return b * nblocks + jnp.minimum((i + 1) * nqb, nblocks - 1)

    return pl.pallas_call(
        functools.partial(_swa_attn_kernel, nqb=nqb, nblocks=nblocks),
        grid=(batch, nsteps),
        in_specs=[
            pl.BlockSpec(memory_space=pltpu.SMEM),
            pl.BlockSpec((hq_w, tb), lambda b, i: (0, b * nsteps + i)),
            pl.BlockSpec((BLOCK, kv_w), lambda b, i: (prev_blk(b, i), 0)),
            pl.BlockSpec((tb, kv_w), lambda b, i: (b * nsteps + i, 0)),
            pl.BlockSpec((BLOCK, kv_w), lambda b, i: (next_blk(b, i), 0)),
            pl.BlockSpec((kv_w, BLOCK), lambda b, i: (0, prev_blk(b, i))),
            pl.BlockSpec((kv_w, tb), lambda b, i: (0, b * nsteps + i)),
            pl.BlockSpec((kv_w, BLOCK), lambda b, i: (0, next_blk(b, i))),
            _const_spec(bias.shape),
        ],
        out_specs=pl.BlockSpec((hq_w, tb), lambda b, i: (0, b * nsteps + i)),
        out_shape=jax.ShapeDtypeStruct((hq_w, t), BF16),
        scratch_shapes=[pltpu.VMEM((3 * BLOCK, SWA_GROUP * BLOCK), F32)] * 2,
        compiler_params=_params(("parallel", "parallel")),
        name="swa_attn",
    )(sink, qt, k, k, k, vt, vt, vt, bias)


def _out_proj_kernel(o_ref, w_ref, g_ref, x_ref, y_ref, *, feature_major):
    if feature_major:
        y = lax.dot_general(o_ref[...], w_ref[...], (((0,), (0,)), ((), ())), preferred_element_type=F32)
    else:
        y = _dot(o_ref[...], w_ref[...])
    y_ref[...] = x_ref[...] + _rms(y, g_ref[...])


def _out_proj_residual(o, w, layer, g, x, feature_major=False, tm=512):
    t, d = x.shape
    if feature_major:
        o_spec = pl.BlockSpec((o.shape[0], tm), lambda i: (0, i))
    else:
        o_spec = pl.BlockSpec((tm, o.shape[1]), lambda i: (i, 0))
    return pl.pallas_call(
        functools.partial(_out_proj_kernel, feature_major=feature_major),
        grid=(t // tm,),
        in_specs=[
            o_spec,
            _layer_spec(w, layer),
            _const_spec((1, d)),
            pl.BlockSpec((tm, d), lambda i: (i, 0)),
        ],
        out_specs=pl.BlockSpec((tm, d), lambda i: (i, 0)),
        out_shape=jax.ShapeDtypeStruct((t, d), F32),
        compiler_params=_params(("parallel",)),
        name="out_proj_residual",
    )(o, w, g, x)


def _mlp_kernel(x_ref, gin_ref, wup_ref, wdn_ref, gout_ref, y_ref, h_sc, acc_sc):
    j = pl.program_id(1)
    tm = x_ref.shape[0]

    def ff_step(first):
        u = jnp.maximum(_dot(h_sc[...], wup_ref[...]), 0.0)
        d = _dot((u * u).astype(BF16), wdn_ref[...])
        if first:
            acc_sc[...] = d
        else:
            acc_sc[...] += d

    @pl.when(j == 0)
    def _():
        g = gin_ref[...]
        for r in range(0, tm, BF16_ROWS):
            h_sc[r:r + BF16_ROWS, :] = _rms(x_ref[r:r + BF16_ROWS, :], g).astype(BF16)
        ff_step(True)

    pl.when(j > 0)(functools.partial(ff_step, False))

    @pl.when(j == pl.num_programs(1) - 1)
    def _():
        g = gout_ref[...]
        for r in range(0, tm, F32_ROWS):
            y_ref[r:r + F32_ROWS, :] = x_ref[r:r + F32_ROWS, :] + _rms(acc_sc[r:r + F32_ROWS, :], g)


def _mlp(x, gin, wup, wdn, gout, layer, tm=512, tf=1024):
    t, d = x.shape
    f = wup.shape[-1]
    return pl.pallas_call(
        _mlp_kernel,
        grid=(t // tm, f // tf),
        in_specs=[
            pl.BlockSpec((tm, d), lambda i, j: (i, 0)),
            _const_spec((1, d)),
            pl.BlockSpec((None, d, tf), lambda i, j: (layer, 0, j)),
            pl.BlockSpec((None, tf, d), lambda i, j: (layer, j, 0)),
            _const_spec((1, d)),
        ],
        out_specs=pl.BlockSpec((tm, d), lambda i, j: (i, 0)),
        out_shape=jax.ShapeDtypeStruct((t, d), F32),
        scratch_shapes=[pltpu.VMEM((tm, d), BF16), pltpu.VMEM((tm, d), F32)],
        compiler_params=_params(("parallel", "arbitrary")),
        name="mlp",
    )(x, gin, wup, wdn, gout)


def _ple_kernel(x_ref, p_ref, wpu_ref, pn_ref, wg_ref, y_ref):
    x = x_ref[...]
    e = _rms(_dot(p_ref[...].astype(BF16), wpu_ref[...]), pn_ref[...])
    z = _dot(x.astype(BF16), wg_ref[...])
    y_ref[...] = x + e / (1.0 + jnp.exp(-z))


def _ple(x, p, wpu, pn, wg, layer, tm=512):
    t, d = x.shape
    return pl.pallas_call(
        _ple_kernel,
        grid=(t // tm,),
        in_specs=[
            pl.BlockSpec((tm, d), lambda i: (i, 0)),
            pl.BlockSpec((None, tm, p.shape[-1]), lambda i: (layer, i, 0)),
            _layer_spec(wpu, layer),
            _const_spec((1, d)),
            _layer_spec(wg, layer),
        ],
        out_specs=pl.BlockSpec((tm, d), lambda i: (i, 0)),
        out_shape=jax.ShapeDtypeStruct((t, d), F32),
        compiler_params=_params(("parallel",)),
        name="ple",
    )(x, p, wpu, pn, wg)


def _pad_cols(w, width):
    return jnp.pad(w, ((0, 0), (0, width - w.shape[1])))


def _swap_halves(w):
    half = w.shape[1] // 2
    return jnp.concatenate([w[:, half:], w[:, :half]], axis=1)


def _prep_mla_weights(w_down, w_uq):
    base = Q_LORA + KV_LORA
    w_kr = w_down[:, base:]
    wd = jnp.concatenate([w_down[:, :base], _pad_cols(w_kr, LANES), _pad_cols(_swap_halves(w_kr), LANES)], axis=1)
    wq = w_uq.reshape(Q_LORA, MLA_HEADS, NOPE_DIM + ROPE_DIM)
    nope = wq[:, :, :NOPE_DIM].reshape(Q_LORA, MLA_HEADS * NOPE_DIM)
    rope = wq[:, :, NOPE_DIM:]
    half = ROPE_DIM // 2
    swapped = jnp.concatenate([rope[:, :, half:], rope[:, :, :half]], axis=2)
    rope = rope.reshape(Q_LORA, MLA_HEADS * ROPE_DIM)
    swapped = swapped.reshape(Q_LORA, MLA_HEADS * ROPE_DIM)
    return wd.astype(BF16), jnp.concatenate([nope, rope, swapped], axis=1).astype(BF16)


def _rope_tables(seq):
    half = ROPE_DIM // 2
    inv = 1.0 / (ROPE_THETA ** (jnp.arange(half, dtype=F32) / half))
    ang = jnp.arange(seq).astype(F32)[:, None] * inv[None, :]
    cos = jnp.cos(ang)
    sin = jnp.sin(ang)
    return jnp.tile(cos, (1, LANES // half)), jnp.tile(jnp.concatenate([-sin, sin], axis=1), (1, LANES // ROPE_DIM))


def _trunk(x, p, batch, seq, w):
    cos, sin = _rope_tables(seq)
    for i in range(DEPTH):
        g = w["norm_gains"][i]
        j = i // N_MIXERS
        if i % N_MIXERS == 0:
            q, k, v = _mla_proj(x, g[0:1], w["mla_w_down"][j], w["mla_q_norm"][j:j + 1], w["mla_kv_norm"][j:j + 1],
                                w["mla_w_uq"][j], w["mla_w_ukv"], j, cos, sin, seq)
            o = _mla_attn(q, k, v, batch, seq)
            x = _out_proj_residual(o, w["mla_w_o"], j, g[1:2], x)
        else:
            qt, k, vt = _swa_proj(x, g[0:1], w["swa_w_qkv"], j)
            ot = _swa_attn(qt, k, vt, w["swa_sink"][j], w["bias_table"], batch, seq)
            x = _out_proj_residual(ot, w["swa_w_o"], j, g[1:2], x, feature_major=True)
        x = _mlp(x, g[2:3], w["mlp_w_up"], w["mlp_w_down"], g[3:4], i)
        x = _ple(x, p, w["ple_w_up"], w["ple_norm"][i:i + 1], w["ple_w_gate"], i)
    return x


def _prep_weights(norm_gains, mla_w_down, mla_q_norm, mla_kv_norm, mla_w_uq, mla_w_ukv, mla_w_o, swa_w_qkv,
                  swa_sink, swa_w_o, rel_bias, mlp_w_up, mlp_w_down, ple_w_up, ple_w_gate, ple_norm):
    wd, wq = zip(*[_prep_mla_weights(mla_w_down[j], mla_w_uq[j]) for j in range(mla_w_down.shape[0])])
    return dict(
        norm_gains=norm_gains, mla_w_down=wd, mla_q_norm=mla_q_norm, mla_kv_norm=mla_kv_norm, mla_w_uq=wq,
        mla_w_ukv=mla_w_ukv.astype(BF16), mla_w_o=mla_w_o.astype(BF16), swa_w_qkv=swa_w_qkv.astype(BF16),
        swa_sink=swa_sink, swa_w_o=swa_w_o.astype(BF16), bias_table=_bias_table(rel_bias),
        mlp_w_up=mlp_w_up.astype(BF16), mlp_w_down=mlp_w_down.astype(BF16), ple_w_up=ple_w_up.astype(BF16),
        ple_w_gate=ple_w_gate.astype(BF16), ple_norm=ple_norm)


def kernel(x_prompt, x_sample, p_prompt, p_sample, norm_gains, mla_w_down, mla_q_norm, mla_kv_norm, mla_w_uq,
           mla_w_ukv, mla_w_o, swa_w_qkv, swa_sink, swa_w_o, rel_bias, mlp_w_up, mlp_w_down, ple_w_up, ple_w_gate,
           ple_norm):
    w = _prep_weights(norm_gains, mla_w_down, mla_q_norm, mla_kv_norm, mla_w_uq, mla_w_ukv, mla_w_o, swa_w_qkv,
                      swa_sink, swa_w_o, rel_bias, mlp_w_up, mlp_w_down, ple_w_up, ple_w_gate, ple_norm)
    outs = []
    for x, p in ((x_prompt, p_prompt), (x_sample, p_sample)):
        b, s, d = x.shape
        y = _trunk(x.reshape(b * s, d), p.reshape(DEPTH, b * s, p.shape[-1]), b, s, w)
        outs.append(y.reshape(b, s, d))
    return tuple(outs)
```
